```python
import jax, jax.numpy as jnp
from jax import lax
import numpy as np

D_MODEL = 2048
BATCH = 2
SEQ = 4096
DEPTH = 4
DEC_BATCH = 8
DEC_SEQ = 4
PAST_LEN = 16384
PAGE_SIZE = 128

GLA_HEADS = 4
GLA_DK = 128
GLA_DV = 256
GLA_RANK = 16
GLA_TAU = 16.0
GLA_CHUNK = 64
NSA_HEADS = 16
NSA_KV_HEADS = 4
NSA_GROUP = NSA_HEADS // NSA_KV_HEADS
NSA_DH = 64
CMP_BLOCK = 64
SEL_BLOCK = CMP_BLOCK
TOP_N = 16
WINDOW = 512
PHI_HIDDEN = 128
Q_BLOCK = 128
ROPE_THETA = 500000.0
ROPE_DIMS = NSA_DH // 4
D_FF = 5632
N_EXPERTS = 8
TOP_K = 2
D_FF_EXPERT = D_FF // 2
N_DENSE = (DEPTH + 1) // 2
N_MOE = DEPTH // 2
IN_COLS = 2 * GLA_HEADS * GLA_DK + 2 * GLA_HEADS * GLA_DV + GLA_RANK + NSA_HEADS * NSA_DH + 6 * NSA_KV_HEADS * NSA_DH + 3 * NSA_HEADS + 2 * D_MODEL
EPS = 1e-6
NEG_INF = -1e30
FORCED_SCORE = 1e4

kernel_name = 'gla_nsa_parallel_hybrid_step'


def rms_norm(x, g):
    xf = x.astype(jnp.float32)
    y = xf * lax.rsqrt(jnp.mean(xf * xf, axis=-1, keepdims=True) + EPS)
    return (y * g.astype(jnp.float32)).astype(x.dtype)


def modulate(x, g, shift, scale):
    return rms_norm(x, g) * (1.0 + scale[:, None]) + shift[:, None]


def rope_partial(x, pos):
    half = ROPE_DIMS // 2
    inv = ROPE_THETA ** (-(jnp.arange(half, dtype=jnp.float32) * 2.0 / ROPE_DIMS))
    ang = pos.astype(jnp.float32)[:, None] * inv[None, :]
    shp = (1, pos.shape[0]) + (1,) * (x.ndim - 3) + (half,)
    cos, sin = jnp.cos(ang).reshape(shp), jnp.sin(ang).reshape(shp)
    x1 = x[..., :half].astype(jnp.float32)
    x2 = x[..., half:ROPE_DIMS].astype(jnp.float32)
    rot = jnp.concatenate([x1 * cos - x2 * sin, x2 * cos + x1 * sin], axis=-1).astype(x.dtype)
    return jnp.concatenate([rot, x[..., ROPE_DIMS:]], axis=-1)


def split_in(z):
    sizes = (GLA_HEADS * GLA_DK, GLA_HEADS * GLA_DK, GLA_HEADS * GLA_DV, GLA_RANK, GLA_HEADS * GLA_DV,
             NSA_HEADS * NSA_DH, 6 * NSA_KV_HEADS * NSA_DH, 3 * NSA_HEADS, 2 * D_MODEL)
    return jnp.split(z, [int(i) for i in np.cumsum(sizes)[:-1]], axis=-1)


def project(h, pos, w_in, w_alpha2, b_alpha, qk_g):
    b, t, _ = h.shape
    gq, gk, gv, ga, gr, nq, nkv, ng, mg = split_in(h @ w_in)
    q_gla = gq.reshape(b, t, GLA_HEADS, GLA_DK) * (GLA_DK ** -0.5)
    k_gla = gk.reshape(b, t, GLA_HEADS, GLA_DK)
    v_gla = gv.reshape(b, t, GLA_HEADS, GLA_DV)
    log_a = jax.nn.log_sigmoid((ga @ w_alpha2 + b_alpha).astype(jnp.float32)).reshape(b, t, GLA_HEADS, GLA_DK) / GLA_TAU
    q_nsa = rope_partial(rms_norm(nq.reshape(b, t, NSA_HEADS, NSA_DH), qk_g[0]), pos)
    kv = nkv.reshape(b, t, 3, 2, NSA_KV_HEADS, NSA_DH)
    k3 = rope_partial(rms_norm(kv[:, :, :, 0], qk_g[1:4][:, None, :]), pos)
    rows = jnp.stack([k3, kv[:, :, :, 1]], axis=3)
    cache_rows = rows[:, :, :2].reshape(b, t, 4, NSA_KV_HEADS, NSA_DH)
    win_rows = rows[:, :, 2]
    nsa_g = jax.nn.sigmoid(ng).reshape(b, t, 3, NSA_HEADS)
    merge_g = jax.nn.sigmoid(mg).reshape(b, t, 2, D_MODEL)
    return q_gla, k_gla, v_gla, log_a, gr, q_nsa, cache_rows, win_rows, nsa_g, merge_g


def gla_chunked(q, k, v, log_a, s0):
    b, t = q.shape[:2]
    c = min(GLA_CHUNK, t)
    nc = t // c

    def chunks(x):
        return jnp.moveaxis(x.reshape((b, nc, c) + x.shape[2:]), 1, 0)

    causal = jnp.tril(jnp.ones((c, c), dtype=bool))[None, :, :, None, None]

    def step(state, inp):
        qc, kc, vc, ac = inp
        qc, kc, vc = qc.astype(jnp.float32), kc.astype(jnp.float32), vc.astype(jnp.float32)
        cum = jnp.cumsum(ac, axis=1)
        decay = jnp.exp(jnp.where(causal, cum[:, :, None] - cum[:, None, :], NEG_INF))
        scores = jnp.einsum('bthk,bshk,btshk->bhts', qc, kc, decay)
        o = jnp.einsum('bhts,bshv->bthv', scores, vc) + jnp.einsum('bthk,bhkv->bthv', qc * jnp.exp(cum), state)
        last = cum[:, -1]
        state = jnp.exp(last)[..., None] * state + jnp.einsum('bshk,bshv->bhkv', kc * jnp.exp(last[:, None] - cum), vc)
        return state, o

    s_fin, o = lax.scan(step, s0, (chunks(q), chunks(k), chunks(v), chunks(log_a)))
    return jnp.moveaxis(o, 0, 1).reshape(b, t, GLA_HEADS, GLA_DV), s_fin


def grouped_attention(q, k, v, mask):
    n, tq, _, dh = q.shape
    qg = q.reshape(n, tq, NSA_KV_HEADS, NSA_GROUP, dh)
    s = jnp.einsum('nqhgd,nkhd->nhgqk', qg, k).astype(jnp.float32) * (dh ** -0.5)
    m = mask[:, None, None]
    p = jax.nn.softmax(jnp.where(m, s, NEG_INF), axis=-1)
    p = jnp.where(m.any(axis=-1, keepdims=True), p, 0.0)
    o = jnp.einsum('nhgqk,nkhd->nqhgd', p.astype(v.dtype), v).reshape(n, tq, NSA_HEADS, dh)
    return o, p


def compress(rows, pe, w1, w2):
    b, t = rows.shape[:2]
    nf = t // CMP_BLOCK
    blk = rows[:, :nf * CMP_BLOCK].reshape(b, nf, CMP_BLOCK, NSA_KV_HEADS, NSA_DH) + pe[None, None, :, None, :]
    flat = blk.transpose(0, 1, 3, 2, 4).reshape(b, nf, NSA_KV_HEADS, CMP_BLOCK * NSA_DH)
    return jax.nn.silu(flat @ w1) @ w2


def selected_attention(q, q_pos, k_rows, v_rows, importance):
    b, t_tot = k_rows.shape[:2]
    tq = q.shape[1]
    nb = -(-t_tot // SEL_BLOCK)
    pad = nb * SEL_BLOCK - t_tot

    def blocks(x):
        x = jnp.pad(x, ((0, 0), (0, pad), (0, 0), (0, 0)))
        return x.reshape(b, nb, SEL_BLOCK, NSA_KV_HEADS, NSA_DH).transpose(0, 3, 1, 2, 4)

    kb, vb = blocks(k_rows), blocks(v_rows)
    imp = jnp.pad(importance, ((0, 0), (0, 0), (0, 0), (0, nb - importance.shape[-1])))
    blk = jnp.arange(nb)[None, :]
    cur = (q_pos // SEL_BLOCK)[:, None]
    forced = (blk == 0) | (blk == cur) | (blk == cur - 1)
    score = jnp.where(blk <= cur, jnp.where(forced, FORCED_SCORE, imp), NEG_INF)
    n_sel = min(TOP_N, nb)
    top_s, idx = lax.top_k(score, n_sel)
    valid = top_s > 0.5 * NEG_INF
    qb = min(Q_BLOCK, tq)
    nqb = tq // qb
    q_blocks = q.reshape(b, nqb, qb, NSA_KV_HEADS, NSA_GROUP, NSA_DH).transpose(1, 0, 3, 4, 2, 5)
    idx_blocks = idx.reshape(b, NSA_KV_HEADS, nqb, qb, n_sel).transpose(2, 0, 1, 3, 4)
    valid_blocks = valid.reshape(b, NSA_KV_HEADS, nqb, qb, n_sel).transpose(2, 0, 1, 3, 4)
    pos_blocks = q_pos.reshape(nqb, qb)
    bi = jnp.arange(b)[:, None, None, None]
    hi = jnp.arange(NSA_KV_HEADS)[None, :, None, None]

    def one_block(args):
        q_blk, i_blk, ok_blk, p_blk = args
        kg = kb[bi, hi, i_blk]
        vg = vb[bi, hi, i_blk]
        kpos = i_blk[..., None] * SEL_BLOCK + jnp.arange(SEL_BLOCK)
        mask = ok_blk[..., None] & (kpos <= p_blk[None, None, :, None, None])
        s = jnp.einsum('bhgqd,bhqnsd->bhgqns', q_blk, kg).astype(jnp.float32) * (NSA_DH ** -0.5)
        s = jnp.where(mask[:, :, None], s, NEG_INF).reshape(b, NSA_KV_HEADS, NSA_GROUP, qb, n_sel * SEL_BLOCK)
        p = jax.nn.softmax(s, axis=-1).astype(vg.dtype)
        return jnp.einsum('bhgqm,bhqmd->bhgqd', p, vg.reshape(b, NSA_KV_HEADS, qb, n_sel * SEL_BLOCK, NSA_DH))

    o = lax.map(one_block, (q_blocks, idx_blocks, valid_blocks, pos_blocks))
    return o.transpose(1, 0, 4, 2, 3, 5).reshape(b, tq, NSA_HEADS, NSA_DH)


def nsa_cmp_slc(q, q_pos, rows, phi_pe, phi_w1, phi_w2, g_kc):
    kc = rms_norm(compress(rows[:, :, 0], phi_pe[0], phi_w1[0], phi_w2[0]), g_kc)
    vc = compress(rows[:, :, 1], phi_pe[1], phi_w1[1], phi_w2[1])
    nf = kc.shape[1]
    blk_end = (jnp.arange(nf) + 1) * CMP_BLOCK - 1
    o_cmp, p_cmp = grouped_attention(q, kc, vc, (blk_end[None, :] <= q_pos[:, None])[None])
    o_slc = selected_attention(q, q_pos, rows[:, :, 2], rows[:, :, 3], p_cmp.sum(axis=2))
    return o_cmp, o_slc


def window_prompt(q, k, v):
    b, t = q.shape[:2]
    nqb = t // Q_BLOCK
    nw = WINDOW // Q_BLOCK

    def band(x):
        xb = jnp.pad(x, ((0, 0), (WINDOW, 0), (0, 0), (0, 0))).reshape(b, nqb + nw, Q_BLOCK, NSA_KV_HEADS, NSA_DH)
        xb = jnp.concatenate([xb[:, j:j + nqb] for j in range(nw + 1)], axis=2)
        return xb.reshape(b * nqb, (nw + 1) * Q_BLOCK, NSA_KV_HEADS, NSA_DH)

    qp = jnp.arange(t).reshape(nqb, Q_BLOCK)[:, :, None]
    kp = ((jnp.arange(nqb) * Q_BLOCK - WINDOW)[:, None] + jnp.arange((nw + 1) * Q_BLOCK)[None, :])[:, None, :]
    mask = (kp >= 0) & (kp <= qp) & (kp >= qp - WINDOW)
    o, _ = grouped_attention(q.reshape(b * nqb, Q_BLOCK, NSA_HEADS, NSA_DH), band(k), band(v), jnp.tile(mask, (b, 1, 1)))
    return o.reshape(b, t, NSA_HEADS, NSA_DH)


def mix_output(o_gla, r, o_cmp, o_slc, o_win, nsa_g, merge_g, gla_norm_g, w_branch_a, w_branch_b, w_out):
    b, t = r.shape[:2]
    og = rms_norm(o_gla.astype(r.dtype), gla_norm_g.reshape(GLA_HEADS, GLA_DV)).reshape(b, t, -1) * jax.nn.silu(r)
    on = (nsa_g[..., 0, :, None] * o_cmp + nsa_g[..., 1, :, None] * o_slc + nsa_g[..., 2, :, None] * o_win).reshape(b, t, -1)
    merged = merge_g[:, :, 0] * (og @ w_branch_a) + merge_g[:, :, 1] * (on @ w_branch_b)
    return merged @ w_out


def swiglu(h, wg, wu, wd):
    return (jax.nn.silu(h @ wg) * (h @ wu)) @ wd


def moe_swiglu(h, w_router, wg, wu, wd):
    logits = (h @ w_router).astype(jnp.float32)
    top_v, top_i = lax.top_k(logits, TOP_K)
    gates = jnp.sum(jax.nn.one_hot(top_i, N_EXPERTS, dtype=jnp.float32) * jax.nn.softmax(top_v, axis=-1)[..., None], axis=-2)
    y = jnp.zeros_like(h)
    for e in range(N_EXPERTS):
        y = y + gates[..., e:e + 1].astype(h.dtype) * swiglu(h, wg[e], wu[e], wd[e])
    return y


def channel_sublayer(x, mod, l, g, w_ffn_gate, w_ffn_up, w_ffn_down, w_router, w_exp_gate, w_exp_up, w_exp_down):
    h = modulate(x, g, mod[:, 1, 0], mod[:, 1, 1])
    if l % 2 == 0:
        f = swiglu(h, w_ffn_gate[l // 2], w_ffn_up[l // 2], w_ffn_down[l // 2])
    else:
        f = moe_swiglu(h, w_router[l // 2], w_exp_gate[l // 2], w_exp_up[l // 2], w_exp_down[l // 2])
    return x + mod[:, 1, 2][:, None] * f


def setup_inputs(seed: int = 0) -> dict:
    key = jax.random.key(seed)
    ks = iter(jax.random.split(key, 40))

    def nrm(shape, scale):
        return jax.random.normal(next(ks), shape, jnp.float32) * scale

    n_pages = PAST_LEN // PAGE_SIZE
    n_used = DEC_BATCH * n_pages
    n_pool = (n_used * 5) // 4
    win_buf = min(WINDOW, PAST_LEN)
    d = D_MODEL
    inp = {}
    inp['x_prompt'] = nrm((BATCH, SEQ, d), 1.0)
    inp['x_sample'] = nrm((DEC_BATCH, DEC_SEQ, d), 1.0)
    inp['cache_kv'] = nrm((DEPTH, n_pool, PAGE_SIZE, 4, NSA_KV_HEADS, NSA_DH), 1.0)
    inp['state_kv_win'] = nrm((DEPTH, DEC_BATCH, win_buf, 2, NSA_KV_HEADS, NSA_DH), 1.0)
    inp['state_gla'] = nrm((DEPTH, DEC_BATCH, GLA_HEADS, GLA_DK, GLA_DV), 1.0)
    inp['page_table'] = jax.random.permutation(next(ks), n_pool)[:n_used].reshape(DEC_BATCH, n_pages).astype(jnp.int32)
    inp['c_prompt'] = nrm((BATCH, d), 1.0)
    inp['c_sample'] = nrm((DEC_BATCH, d), 1.0)
    inp['norm_g'] = 1.0 + nrm((DEPTH, 2, d), 0.01)
    inp['w_ada'] = nrm((DEPTH, d, 6 * d), 0.5 * d ** -0.5)
    inp['b_ada'] = nrm((DEPTH, 6 * d), 0.01)
    inp['w_in'] = nrm((DEPTH, d, IN_COLS), d ** -0.5)
    inp['w_alpha2'] = nrm((DEPTH, GLA_RANK, GLA_HEADS * GLA_DK), GLA_RANK ** -0.5)
    inp['b_alpha'] = nrm((DEPTH, GLA_HEADS * GLA_DK), 0.1)
    inp['gla_norm_g'] = 1.0 + nrm((DEPTH, GLA_HEADS * GLA_DV), 0.01)
    inp['qk_g'] = 1.0 + nrm((DEPTH, 5, NSA_DH), 0.01)
    inp['phi_pe'] = nrm((DEPTH, 2, CMP_BLOCK, NSA_DH), 0.1)
    inp['phi_w1'] = nrm((DEPTH, 2, CMP_BLOCK * NSA_DH, PHI_HIDDEN), (CMP_BLOCK * NSA_DH) ** -0.5)
    inp['phi_w2'] = nrm((DEPTH, 2, PHI_HIDDEN, NSA_DH), PHI_HIDDEN ** -0.5)
    inp['w_branch_a'] = nrm((DEPTH, GLA_HEADS * GLA_DV, d), (GLA_HEADS * GLA_DV) ** -0.5)
    inp['w_branch_b'] = nrm((DEPTH, NSA_HEADS * NSA_DH, d), (NSA_HEADS * NSA_DH) ** -0.5)
    inp['w_out'] = nrm((DEPTH, d, d), d ** -0.5)
    inp['w_ffn_gate'] = nrm((N_DENSE, d, D_FF), d ** -0.5)
    inp['w_ffn_up'] = nrm((N_DENSE, d, D_FF), d ** -0.5)
    inp['w_ffn_down'] = nrm((N_DENSE, D_FF, d), D_FF ** -0.5)
    inp['w_router'] = nrm((N_MOE, d, N_EXPERTS), d ** -0.5)
    inp['w_exp_gate'] = nrm((N_MOE, N_EXPERTS, d, D_FF_EXPERT), d ** -0.5)
    inp['w_exp_up'] = nrm((N_MOE, N_EXPERTS, d, D_FF_EXPERT), d ** -0.5)
    inp['w_exp_down'] = nrm((N_MOE, N_EXPERTS, D_FF_EXPERT, d), D_FF_EXPERT ** -0.5)
    return inp


def reference(x_prompt, x_sample, cache_kv, state_kv_win, state_gla, page_table, c_prompt, c_sample,
              norm_g, w_ada, b_ada, w_in, w_alpha2, b_alpha, gla_norm_g, qk_g, phi_pe, phi_w1, phi_w2,
              w_branch_a, w_branch_b, w_out, w_ffn_gate, w_ffn_up, w_ffn_down,
              w_router, w_exp_gate, w_exp_up, w_exp_down):
    b_p, t_p, _ = x_prompt.shape
    b_s, t_s, _ = x_sample.shape
    n_pages = page_table.shape[1]
    past_len = n_pages * cache_kv.shape[2]
    win_buf = state_kv_win.shape[2]
    pos_p = jnp.arange(t_p, dtype=jnp.int32)
    pos_s = past_len + jnp.arange(t_s, dtype=jnp.int32)
    pos_w = past_len - win_buf + jnp.arange(win_buf + t_s, dtype=jnp.int32)
    win_mask_s = ((pos_w[None, :] <= pos_s[:, None]) & (pos_w[None, :] >= pos_s[:, None] - WINDOW))[None]
    x_p, x_s = x_prompt, x_sample
    kv_p, win_p, gla_p, kv_s, win_s, gla_s = [], [], [], [], [], []
    for l in range(DEPTH):
        mod_p = (c_prompt @ w_ada[l] + b_ada[l]).reshape(b_p, 2, 3, D_MODEL)
        mod_s = (c_sample @ w_ada[l] + b_ada[l]).reshape(b_s, 2, 3, D_MODEL)
        h = modulate(x_p, norm_g[l, 0], mod_p[:, 0, 0], mod_p[:, 0, 1])
        q_g, k_g, v_g, la, r, q_n, rows, wrows, g_n, g_m = project(h, pos_p, w_in[l], w_alpha2[l], b_alpha[l], qk_g[l])
        o_gla, st = gla_chunked(q_g, k_g, v_g, la, jnp.zeros((b_p, GLA_HEADS, GLA_DK, GLA_DV), jnp.float32))
        o_cmp, o_slc = nsa_cmp_slc(q_n, pos_p, rows, phi_pe[l], phi_w1[l], phi_w2[l], qk_g[l, 4])
        o_win = window_prompt(q_n, wrows[:, :, 0], wrows[:, :, 1])
        x_p = x_p + mod_p[:, 0, 2][:, None] * mix_output(o_gla, r, o_cmp, o_slc, o_win, g_n, g_m, gla_norm_g[l], w_branch_a[l], w_branch_b[l], w_out[l])
        kv_p.append(rows)
        win_p.append(jnp.pad(wrows, ((0, 0), (WINDOW, 0), (0, 0), (0, 0), (0, 0)))[:, -WINDOW:])
        gla_p.append(st)
        h = modulate(x_s, norm_g[l, 0], mod_s[:, 0, 0], mod_s[:, 0, 1])
        q_g, k_g, v_g, la, r, q_n, rows, wrows, g_n, g_m = project(h, pos_s, w_in[l], w_alpha2[l], b_alpha[l], qk_g[l])
        o_gla, st = gla_chunked(q_g, k_g, v_g, la, state_gla[l].astype(jnp.float32))
        past = cache_kv[l][page_table].reshape(b_s, past_len, 4, NSA_KV_HEADS, NSA_DH).astype(rows.dtype)
        o_cmp, o_slc = nsa_cmp_slc(q_n, pos_s, jnp.concatenate([past, rows], axis=1), phi_pe[l], phi_w1[l], phi_w2[l], qk_g[l, 4])
        w_all = jnp.concatenate([state_kv_win[l].astype(wrows.dtype), wrows], axis=1)
        o_win, _ = grouped_attention(q_n, w_all[:, :, 0], w_all[:, :, 1], win_mask_s)
        x_s = x_s + mod_s[:, 0, 2][:, None] * mix_output(o_gla, r, o_cmp, o_slc, o_win, g_n, g_m, gla_norm_g[l], w_branch_a[l], w_branch_b[l], w_out[l])
        kv_s.append(rows)
        win_s.append(w_all[:, -win_buf:])
        gla_s.append(st)
        x_p = channel_sublayer(x_p, mod_p, l, norm_g[l, 1], w_ffn_gate, w_ffn_up, w_ffn_down, w_router, w_exp_gate, w_exp_up, w_exp_down)
        x_s = channel_sublayer(x_s, mod_s, l, norm_g[l, 1], w_ffn_gate, w_ffn_up, w_ffn_down, w_router, w_exp_gate, w_exp_up, w_exp_down)
    return (x_p, x_s, jnp.stack(kv_p), jnp.stack(win_p), jnp.stack(gla_p), jnp.stack(kv_s), jnp.stack(win_s), jnp.stack(gla_s))
```

```python
import functools

import jax
import jax.numpy as jnp
from jax import lax
from jax.experimental import pallas as pl
from jax.experimental.pallas import tpu as pltpu

f32 = jnp.float32
bf16 = jnp.bfloat16
i32 = jnp.int32

GLA_HEADS = 4
GLA_DK = 128
GLA_DV = 256
GLA_RANK = 16
GLA_TAU = 16.0
GLA_CHUNK = 64
GLA_SUB = 16
NSA_HEADS = 16
NSA_KV_HEADS = 4
NSA_DH = 64
CMP_BLOCK = 64
TOP_N = 16
WINDOW = 512
Q_BLOCK = 128
ROPE_THETA = 500000.0
ROPE_DIMS = NSA_DH // 4
N_EXPERTS = 8
TOP_K = 2
EPS = 1e-6
NEG_INF = -1e30
FORCED_SCORE = 1e4

LANES = 128
TS_PAD = 16
VMEM_LIMIT = 56 * 1024 * 1024

C_GQ, C_GK, C_GV, C_GR, C_NQ, C_KVC, C_KVW, C_NG, C_GA, C_MG = 0, 512, 1024, 2048, 3072, 4096, 5120, 5632, 5680, 6144
N_PROJ = 10240
PAGES_PER_STEP = 8


def _cp(sem):
    return pltpu.CompilerParams(dimension_semantics=sem, vmem_limit_bytes=VMEM_LIMIT)


def _dot(a, b):
    return jnp.dot(a, b, preferred_element_type=f32)


def _dot_nt(a, b):
    return lax.dot_general(a, b, (((1,), (1,)), ((), ())), preferred_element_type=f32)


def _dot_tn(a, b):
    return lax.dot_general(a, b, (((0,), (0,)), ((), ())), preferred_element_type=f32)


def _split2(x):
    hi = x.astype(bf16)
    lo = (x - hi.astype(f32)).astype(bf16)
    return hi, lo


def _split3(x):
    hi = x.astype(bf16)
    r = x - hi.astype(f32)
    mid = r.astype(bf16)
    lo = (r - mid.astype(f32)).astype(bf16)
    return hi, mid, lo


def _dotp(fn, a, b, hp):
    if not hp:
        return fn(a.astype(bf16), b.astype(bf16))
    ah, al = _split2(a)
    bh, bl = _split2(b)
    return fn(ah, bh) + fn(al, bh) + fn(ah, bl)


def _mm(a, ws):
    if len(ws) == 1:
        return _dot(a.astype(bf16), ws[0])
    ah, al = _split2(a)
    return _dot(ah, ws[0]) + _dot(al, ws[0]) + _dot(ah, ws[1])


def _silu(x):
    return x * jax.nn.sigmoid(x)


def _group_ones(n, group):
    r = lax.broadcasted_iota(i32, (n, n), 0) // group
    c = lax.broadcasted_iota(i32, (n, n), 1) // group
    return jnp.where(r == c, 1.0, 0.0).astype(bf16)


def _rms64(x, gain):
    e = _group_ones(LANES, NSA_DH)
    hi, lo = _split2(x * x)
    ssq = _dot(hi, e) + _dot(lo, e)
    return x * lax.rsqrt(ssq * (1.0 / NSA_DH) + EPS) * gain


def _modulated(x_ref, g_ref, sh_ref, sc_ref):
    x = x_ref[0]
    y = x * lax.rsqrt(jnp.mean(x * x, axis=-1, keepdims=True) + EPS) * g_ref[...]
    return y * (1.0 + sc_ref[0]) + sh_ref[0]


def _mod_specs(t, tm, d, per_row):
    if per_row:
        return pl.BlockSpec((1, tm, d), lambda g, i, j: (g, i, 0))
    return pl.BlockSpec((1, 1, d), lambda g, i, j: (g, 0, 0))


def _ada_kernel(c_ref, w_ref, b_ref, o_ref):
    o_ref[0] = _dotp(_dot, c_ref[...], w_ref[0], True) + b_ref[0]


def ada_all(c_pad, w_ada, b_ada):
    nl, d, n = w_ada.shape
    tn = 1024
    return pl.pallas_call(
        _ada_kernel,
        out_shape=jax.ShapeDtypeStruct((nl, c_pad.shape[0], n), f32),
        grid=(nl, n // tn),
        in_specs=[pl.BlockSpec(c_pad.shape, lambda l, j: (0, 0)),
                  pl.BlockSpec((1, d, tn), lambda l, j: (l, 0, j)),
                  pl.BlockSpec((1, 1, tn), lambda l, j: (l, 0, j))],
        out_specs=pl.BlockSpec((1, c_pad.shape[0], tn), lambda l, j: (l, 0, j)),
        compiler_params=_cp(("arbitrary", "arbitrary")),
        name="ada",
    )(c_pad, w_ada, b_ada.reshape(nl, 1, n))


def _store_split(h, h_refs):
    if len(h_refs) == 1:
        h_refs[0][...] = h.astype(bf16)
    else:
        hi, lo = _split2(h)
        h_refs[0][...] = hi
        h_refs[1][...] = lo


def _mm_split(h_refs, w_refs):
    acc = _dot(h_refs[0][...], w_refs[0][...])
    if len(w_refs) == 2:
        acc = acc + _dot(h_refs[1][...], w_refs[0][...]) + _dot(h_refs[0][...], w_refs[1][...])
    return acc


def _proj_kernel(*refs, sig_from, nw):
    x_ref, g_ref, sh_ref, sc_ref = refs[:4]
    w_refs = refs[4:4 + nw]
    o_ref = refs[4 + nw]
    h_refs = refs[5 + nw:]
    j = pl.program_id(2)

    @pl.when(j == 0)
    def _():
        _store_split(_modulated(x_ref, g_ref, sh_ref, sc_ref), h_refs)

    acc = _mm_split(h_refs, w_refs)

    @pl.when(j < sig_from)
    def _():
        o_ref[0] = acc

    @pl.when(j >= sig_from)
    def _():
        o_ref[0] = jax.nn.sigmoid(acc)


def proj(x, g, shift, scale, ws):
    grp, t, d = x.shape
    n = ws[0].shape[1]
    tm, tn = min(t, 512), 1024
    per_row = shift.shape[1] != 1
    nw = len(ws)
    return pl.pallas_call(
        functools.partial(_proj_kernel, sig_from=C_MG // tn, nw=nw),
        out_shape=jax.ShapeDtypeStruct((grp, t, n), f32),
        grid=(grp, t // tm, n // tn),
        in_specs=[pl.BlockSpec((1, tm, d), lambda g_, i, j: (g_, i, 0)),
                  pl.BlockSpec((1, d), lambda g_, i, j: (0, 0)),
                  _mod_specs(t, tm, d, per_row), _mod_specs(t, tm, d, per_row)]
        + [pl.BlockSpec((d, tn), lambda g_, i, j: (0, j))] * nw,
        out_specs=pl.BlockSpec((1, tm, tn), lambda g_, i, j: (g_, i, j)),
        scratch_shapes=[pltpu.VMEM((tm, d), bf16)] * nw,
        compiler_params=_cp(("arbitrary", "arbitrary", "arbitrary")),
        name="proj",
    )(x, g, shift, scale, *ws)


def _rope(y, cos, sa, sb):
    return y * cos + pltpu.roll(y, LANES - ROPE_DIMS // 2, 1) * sa + pltpu.roll(y, ROPE_DIMS // 2, 1) * sb


def _nsa_prep_kernel(nq_ref, kvc_ref, kvw_ref, cos_ref, sa_ref, sb_ref, gq_ref, gk_ref, qz_ref, oc_ref, ow_ref):
    cos, sa, sb = cos_ref[...], sa_ref[...], sb_ref[...]
    lane = lax.broadcasted_iota(i32, cos.shape, 1)
    left = lane < NSA_DH
    for m in range(NSA_HEADS // 2):
        y = _rope(_rms64(nq_ref[0, :, m * LANES:(m + 1) * LANES], gq_ref[...]), cos, sa, sb) * (NSA_DH ** -0.5)
        yr = pltpu.roll(y, NSA_DH, 1)
        if (m // 2) % 2 == 0:
            a, b = jnp.where(left, y, 0.0), jnp.where(left, yr, 0.0)
        else:
            a, b = jnp.where(left, 0.0, yr), jnp.where(left, 0.0, y)
        qz_ref[0, :, (2 * m) * LANES:(2 * m + 1) * LANES] = a.astype(qz_ref.dtype)
        qz_ref[0, :, (2 * m + 1) * LANES:(2 * m + 2) * LANES] = b.astype(qz_ref.dtype)
    for c in range(8):
        x = kvc_ref[0, :, c * LANES:(c + 1) * LANES]
        if (c // 2) % 2 == 0:
            x = _rope(_rms64(x, gk_ref[c // 4:c // 4 + 1, :]), cos, sa, sb)
        oc_ref[0, :, c * LANES:(c + 1) * LANES] = x
    for c in range(4):
        x = kvw_ref[0, :, c * LANES:(c + 1) * LANES]
        if c < 2:
            x = _rope(_rms64(x, gk_ref[2:3, :]), cos, sa, sb)
        ow_ref[0, :, c * LANES:(c + 1) * LANES] = x


def nsa_prep(z, cos, sa, sb, gq, gk, q_dtype):
    grp, t, _ = z.shape
    tm = min(t, 256)
    tab = pl.BlockSpec((tm, LANES), lambda g_, i: (i, 0))
    return pl.pallas_call(
        _nsa_prep_kernel,
        out_shape=(jax.ShapeDtypeStruct((grp, t, 2 * NSA_HEADS * NSA_DH), q_dtype),
                   jax.ShapeDtypeStruct((grp, t, 1024), f32),
                   jax.ShapeDtypeStruct((grp, t, 512), f32)),
        grid=(grp, t // tm),
        in_specs=[pl.BlockSpec((1, tm, 1024), lambda g_, i: (g_, i, C_NQ // 1024)),
                  pl.BlockSpec((1, tm, 1024), lambda g_, i: (g_, i, C_KVC // 1024)),
                  pl.BlockSpec((1, tm, 512), lambda g_, i: (g_, i, C_KVW // 512)),
                  tab, tab, tab,
                  pl.BlockSpec((1, LANES), lambda g_, i: (0, 0)),
                  pl.BlockSpec((3, LANES), lambda g_, i: (0, 0))],
        out_specs=(pl.BlockSpec((1, tm, 2 * NSA_HEADS * NSA_DH), lambda g_, i: (g_, i, 0)),
                   pl.BlockSpec((1, tm, 1024), lambda g_, i: (g_, i, 0)),
                   pl.BlockSpec((1, tm, 512), lambda g_, i: (g_, i, 0))),
        compiler_params=_cp(("arbitrary", "arbitrary")),
        name="nsa_prep",
    )(z, z, z, cos, sa, sb, gq, gk)


def _gla_kernel(*refs, chunk, sub, n_valid, has_init, hp):
    if has_init:
        q_ref, k_ref, v_ref, ga_ref, wa_ref, ba_ref, s0_ref, o_ref, so_ref, st_ref = refs
    else:
        q_ref, k_ref, v_ref, ga_ref, wa_ref, ba_ref, o_ref, so_ref, st_ref = refs
    i = pl.program_id(1)
    tb = q_ref.shape[1]
    ns = chunk // sub

    @pl.when(i == 0)
    def _():
        for h in range(GLA_HEADS):
            if has_init:
                st_ref[h] = s0_ref[0, h].T
            else:
                st_ref[h] = jnp.zeros((GLA_DV, GLA_DK), f32)

    ri = lax.broadcasted_iota(i32, (chunk, chunk), 0)
    ci = lax.broadcasted_iota(i32, (chunk, chunk), 1)
    ci_sub = lax.broadcasted_iota(i32, (sub, chunk), 1)
    tri =jnp.where(ri >= ci, 1.0, 0.0).astype(bf16)
    rrow = lax.broadcasted_iota(i32, (chunk, GLA_DK), 0)
    rmod = rrow % sub
    ga_off = C_GA % LANES

    def do_chunk(c, carry):
        r0 = pl.multiple_of(c * chunk, chunk)
        ga = ga_ref[0, pl.ds(r0, chunk), :][:, ga_off:ga_off + GLA_RANK]
        for h in range(GLA_HEADS):
            hs = slice(h * GLA_DK, (h + 1) * GLA_DK)
            q = q_ref[0, pl.ds(r0, chunk), hs] * (GLA_DK ** -0.5)
            k = k_ref[0, pl.ds(r0, chunk), hs]
            v = v_ref[0, pl.ds(r0, chunk), h * GLA_DV:(h + 1) * GLA_DV]
            x = _dotp(_dot, ga, wa_ref[:, hs], hp) + ba_ref[:, hs]
            la = (jnp.minimum(x, 0.0) - jnp.log1p(jnp.exp(-jnp.abs(x)))) * (1.0 / GLA_TAU)
            if n_valid < chunk:
                la = jnp.where(rrow < n_valid, la, 0.0)
                k = jnp.where(rrow < n_valid, k, 0.0)
            l1, l2, l3 = _split3(la)
            cum = _dot(tri, l1) + _dot(tri, l2) + _dot(tri, l3)
            parts = [jnp.zeros((sub, chunk), f32)]
            for sb_i in range(1, ns):
                lo = sb_i * sub
                m_i = cum[lo:lo + 1, :]
                q_i = q[lo:lo + sub] * jnp.exp(cum[lo:lo + sub] - m_i)
                k_i = k * jnp.exp(jnp.minimum(m_i - cum, 0.0))
                parts.append(jnp.where(ci_sub < lo, _dotp(_dot_nt, q_i, k_i, hp), 0.0))
            scores = parts[0] if ns == 1 else jnp.concatenate(parts, axis=0)
            for j in range(min(sub, n_valid)):
                kj = [jnp.broadcast_to(k[s * sub + j:s * sub + j + 1, :], (sub, GLA_DK)) for s in range(ns)]
                cj = [jnp.broadcast_to(cum[s * sub + j:s * sub + j + 1, :], (sub, GLA_DK)) for s in range(ns)]
                kj = kj[0] if ns == 1 else jnp.concatenate(kj, axis=0)
                cj = cj[0] if ns == 1 else jnp.concatenate(cj, axis=0)
                e = jnp.exp(jnp.where(rmod >= j, cum - cj, NEG_INF))
                val = jnp.sum(q * kj * e, axis=1, keepdims=True)
                scores = jnp.where(ci == (ri // sub) * sub + j, val, scores)
            st = st_ref[h]
            o = _dotp(_dot, scores, v, hp) + _dotp(_dot_nt, q * jnp.exp(cum), st, hp)
            o_ref[0, pl.ds(r0, chunk), h * GLA_DV:(h + 1) * GLA_DV] = o
            last = cum[chunk - 1:chunk, :]
            st_ref[h] = st * jnp.exp(last) + _dotp(_dot_tn, v, k * jnp.exp(last - cum), hp)
        return carry

    lax.fori_loop(0, tb // chunk, do_chunk, 0)

    @pl.when(i == pl.num_programs(1) - 1)
    def _():
        for h in range(GLA_HEADS):
            so_ref[0, h] = st_ref[h].T


def gla(z, wa, ba, s0, chunk, sub, n_valid, hp):
    b, t, _ = z.shape
    tb = min(t, 256)
    has_init = s0 is not None
    hk = GLA_HEADS * GLA_DK
    in_specs = [pl.BlockSpec((1, tb, hk), lambda b_, i: (b_, i, C_GQ // hk)),
                pl.BlockSpec((1, tb, hk), lambda b_, i: (b_, i, C_GK // hk)),
                pl.BlockSpec((1, tb, GLA_HEADS * GLA_DV), lambda b_, i: (b_, i, C_GV // (GLA_HEADS * GLA_DV))),
                pl.BlockSpec((1, tb, LANES), lambda b_, i: (b_, i, C_GA // LANES)),
                pl.BlockSpec((GLA_RANK, hk), lambda b_, i: (0, 0)),
                pl.BlockSpec((1, hk), lambda b_, i: (0, 0))]
    args = [z, z, z, z, wa, ba]
    if has_init:
        in_specs.append(pl.BlockSpec((1, GLA_HEADS, GLA_DK, GLA_DV), lambda b_, i: (b_, 0, 0, 0)))
        args.append(s0)
    return pl.pallas_call(
        functools.partial(_gla_kernel, chunk=chunk, sub=sub, n_valid=n_valid, has_init=has_init, hp=hp),
        out_shape=(jax.ShapeDtypeStruct((b, t, GLA_HEADS * GLA_DV), f32),
                   jax.ShapeDtypeStruct((b, GLA_HEADS, GLA_DK, GLA_DV), f32)),
        grid=(b, t // tb),
        in_specs=in_specs,
        out_specs=(pl.BlockSpec((1, tb, GLA_HEADS * GLA_DV), lambda b_, i: (b_, i, 0)),
                   pl.BlockSpec((1, GLA_HEADS, GLA_DK, GLA_DV), lambda b_, i: (b_, 0, 0, 0))),
        scratch_shapes=[pltpu.VMEM((GLA_HEADS, GLA_DV, GLA_DK), f32)],
        compiler_params=_cp(("arbitrary", "arbitrary")),
        name="gla",
    )(*args)


def _compress_core(x_refs, pe_ref, w1_refs, w2_refs, gk_ref, kc_ref, vc_ref, nblk):
    for c in range(4):
        br = c // 2

        def body(s, acc, c=c, br=br):
            rows = x_refs[c][pl.ds(s, nblk, stride=CMP_BLOCK), :] + pe_ref[br, pl.ds(s, 1), :]
            return acc + _mm(rows, [w[br, s] for w in w1_refs])

        acc = lax.fori_loop(0, CMP_BLOCK, body, jnp.zeros((nblk, 2 * LANES), f32))
        y = _mm(_silu(acc), [w[br] for w in w2_refs])
        if br == 0:
            kc_ref[0, :, (c % 2) * LANES:(c % 2 + 1) * LANES] = _rms64(y, gk_ref[...])
        else:
            vc_ref[0, :, (c % 2) * LANES:(c % 2 + 1) * LANES] = y


def _compress_prompt_kernel(x0, x1, x2, x3, pe_ref, w1_ref, w2_ref, gk_ref, kc_ref, vc_ref):
    nblk = x0.shape[1] // CMP_BLOCK
    _compress_core([x0.at[0], x1.at[0], x2.at[0], x3.at[0]], pe_ref, (w1_ref,), (w2_ref,), gk_ref, kc_ref, vc_ref, nblk)


def compress_prompt(kvc, pe2, w1c, w2c, gkc):
    b, t, _ = kvc.shape
    nblk = t // CMP_BLOCK
    cst = lambda shape: pl.BlockSpec(shape, lambda b_: (0,) * len(shape))
    xs = [pl.BlockSpec((1, t, LANES), functools.partial(lambda b_, c: (b_, 0, c), c=c)) for c in range(4)]
    return pl.pallas_call(
        _compress_prompt_kernel,
        out_shape=(jax.ShapeDtypeStruct((b, nblk, 256), f32), jax.ShapeDtypeStruct((b, nblk, 256), f32)),
        grid=(b,),
        in_specs=xs + [cst(pe2.shape), cst(w1c.shape), cst(w2c.shape), cst(gkc.shape)],
        out_specs=(pl.BlockSpec((1, nblk, 256), lambda b_: (b_, 0, 0)), pl.BlockSpec((1, nblk, 256), lambda b_: (b_, 0, 0))),
        compiler_params=_cp(("arbitrary",)),
        name="compress_prompt",
    )(kvc, kvc, kvc, kvc, pe2, w1c, w2c, gkc)


def _compress_sample_kernel(pt_ref, *refs, pages_macro):
    pages = refs[:PAGES_PER_STEP]
    pe_ref, w1h_ref, w1l_ref, w2h_ref, w2l_ref, gk_ref, kc_ref, vc_ref, x0, x1, x2, x3 = refs[PAGES_PER_STEP:]
    st = pl.program_id(2)
    xs = [x0, x1, x2, x3]
    psz = pages[0].shape[2]
    for kk in range(PAGES_PER_STEP):
        r0 = pl.multiple_of((st * PAGES_PER_STEP + kk) * psz, psz)
        for c in range(4):
            xs[c][pl.ds(r0, psz), :] = pages[kk][0, 0, :, c * LANES:(c + 1) * LANES]

    @pl.when(st == pl.num_programs(2) - 1)
    def _():
        _compress_core(xs, pe_ref, (w1h_ref, w1l_ref), (w2h_ref, w2l_ref), gk_ref, kc_ref, vc_ref,
                       pages_macro * psz // CMP_BLOCK)


def compress_sample(page_flat, cache4, layer, nb, n_pages, pe2, w1cs, w2cs, gkc):
    psz = cache4.shape[2]
    pages_macro = min(n_pages, 64)
    n_macro = n_pages // pages_macro
    steps = pages_macro // PAGES_PER_STEP
    nblk_macro = pages_macro * psz // CMP_BLOCK
    nblk = n_pages * psz // CMP_BLOCK

    def page_map(b_, m, s, pt, kk):
        return (layer, pt[b_ * n_pages + m * pages_macro + s * PAGES_PER_STEP + kk], 0, 0)

    cst = lambda shape: pl.BlockSpec(shape, lambda b_, m, s, pt: (0,) * len(shape), pipeline_mode=pl.Buffered(1))
    pspecs = [pl.BlockSpec((1, 1, psz, 512), functools.partial(page_map, kk=kk)) for kk in range(PAGES_PER_STEP)]
    ospec = pl.BlockSpec((1, nblk_macro, 256), lambda b_, m, s, pt: (b_, m, 0))
    return pl.pallas_call(
        functools.partial(_compress_sample_kernel, pages_macro=pages_macro),
        out_shape=(jax.ShapeDtypeStruct((nb, nblk, 256), f32), jax.ShapeDtypeStruct((nb, nblk, 256), f32)),
        grid_spec=pltpu.PrefetchScalarGridSpec(
            num_scalar_prefetch=1, grid=(nb, n_macro, steps),
            in_specs=pspecs + [cst(pe2.shape), cst(w1cs[0].shape), cst(w1cs[1].shape),
                               cst(w2cs[0].shape), cst(w2cs[1].shape), cst(gkc.shape)],
            out_specs=(ospec, ospec),
            scratch_shapes=[pltpu.VMEM((pages_macro * psz, LANES), f32) for _ in range(4)]),
        compiler_params=_cp(("arbitrary", "arbitrary", "arbitrary")),
        name="compress_sample",
    )(page_flat, *([cache4] * PAGES_PER_STEP), pe2, *w1cs, *w2cs, gkc)


def _stack_heads(qz, rows):
    return jnp.concatenate([qz[:, h * LANES:(h + 1) * LANES] for h in range(8)], axis=0)


def _unstack_heads(o, rows):
    lane = lax.broadcasted_iota(i32, (rows, LANES), 1)
    left = lane < NSA_DH
    outs = []
    for p in range(4):
        a = o[(2 * p) * rows:(2 * p + 1) * rows]
        b = o[(2 * p + 1) * rows:(2 * p + 2) * rows]
        if p < 2:
            outs.append(jnp.where(left, a, pltpu.roll(b, NSA_DH, 1)))
        else:
            outs.append(jnp.where(left, pltpu.roll(a, NSA_DH, 1), b))
    return jnp.concatenate(outs, axis=1)


def _rank_loop(sc_ref, nblk):
    shape = sc_ref.shape
    blk = lax.broadcasted_iota(i32, shape, 0)
    sc = sc_ref[...]

    def body(j, rank):
        rj = jnp.broadcast_to(sc_ref[pl.ds(j, 1), :], shape)
        beats = (rj > sc) | ((rj == sc) & (blk > j))
        return rank + jnp.where(beats, 1, 0)

    return lax.fori_loop(0, nblk, body, jnp.zeros(shape, i32))


def _cmpsel_prompt_kernel(qz_ref, kc_ref, vc_ref, oc_ref, sel_ref, sc_ref):
    qb = pl.program_id(1)
    tq = qz_ref.shape[1]
    nf = kc_ref.shape[1]
    tpos = qb * tq + lax.broadcasted_iota(i32, (8 * tq, nf), 0) % tq
    blk_end = (lax.broadcasted_iota(i32, (8 * tq, nf), 1) + 1) * CMP_BLOCK - 1
    vis = blk_end <= tpos
    for kp in range(2):
        q = _stack_heads(qz_ref[0, :, kp * 8 * LANES:(kp + 1) * 8 * LANES], tq)
        kc = kc_ref[0, :, kp * LANES:(kp + 1) * LANES].astype(bf16)
        vc = vc_ref[0, :, kp * LANES:(kp + 1) * LANES].astype(bf16)
        s = jnp.where(vis, _dot_nt(q, kc), NEG_INF)
        e = jnp.where(vis, jnp.exp(s - jnp.max(s, axis=1, keepdims=True)), 0.0)
        p = e / jnp.maximum(jnp.sum(e, axis=1, keepdims=True), 1e-30)
        oc_ref[0, :, kp * 4 * LANES:(kp + 1) * 4 * LANES] = _unstack_heads(_dot(p.astype(bf16), vc), tq)
        imp_a = p[0:tq] + p[tq:2 * tq] + p[2 * tq:3 * tq] + p[3 * tq:4 * tq]
        imp_b = p[4 * tq:5 * tq] + p[5 * tq:6 * tq] + p[6 * tq:7 * tq] + p[7 * tq:8 * tq]
        imp_t = jnp.concatenate([imp_a, imp_b], axis=1).T
        blk = lax.broadcasted_iota(i32, (nf, tq), 0)
        cur = (qb * tq + lax.broadcasted_iota(i32, (nf, tq), 1)) // CMP_BLOCK
        forced = (blk == 0) | (blk == cur) | (blk == cur - 1)
        sels = []
        for half in range(2):
            sc = jnp.where(blk <= cur, jnp.where(forced, FORCED_SCORE, imp_t[half * nf:(half + 1) * nf]), NEG_INF)
            sc_ref[...] = sc
            rank = _rank_loop(sc_ref, nf)
            sels.append(jnp.where((rank < TOP_N) & (sc > 0.5 * NEG_INF), 1.0, 0.0))
        sel_ref[0, :, kp * 2 * nf:(kp + 1) * 2 * nf] = jnp.concatenate(sels, axis=0).T.astype(bf16)


def cmpsel_prompt(qz, kc, vc):
    b, t, _ = qz.shape
    nf = kc.shape[1]
    tq = Q_BLOCK
    return pl.pallas_call(
        _cmpsel_prompt_kernel,
        out_shape=(jax.ShapeDtypeStruct((b, t, NSA_HEADS * NSA_DH), f32),
                   jax.ShapeDtypeStruct((b, t, NSA_KV_HEADS * nf), bf16)),
        grid=(b, t // tq),
        in_specs=[pl.BlockSpec((1, tq, 2 * NSA_HEADS * NSA_DH), lambda b_, i: (b_, i, 0)),
                  pl.BlockSpec((1, nf, 256), lambda b_, i: (b_, 0, 0)),
                  pl.BlockSpec((1, nf, 256), lambda b_, i: (b_, 0, 0))],
        out_specs=(pl.BlockSpec((1, tq, NSA_HEADS * NSA_DH), lambda b_, i: (b_, i, 0)),
                   pl.BlockSpec((1, tq, NSA_KV_HEADS * nf), lambda b_, i: (b_, i, 0))),
        scratch_shapes=[pltpu.VMEM((nf, tq), f32)],
        compiler_params=_cp(("arbitrary", "arbitrary")),
        name="cmpsel_prompt",
    )(qz, kc, vc)


def _attn_prompt_kernel(*refs, mode, kt):
    if mode == "slc":
        qz_ref, k_ref, v_ref, sel_ref, o_ref, m_ref, l_ref, acc_ref = refs
    else:
        qz_ref, k_ref, v_ref, o_ref, m_ref, l_ref, acc_ref = refs
    qb = pl.program_id(2)
    tq = qz_ref.shape[1]
    q = _stack_heads(qz_ref[0], tq)
    tpos = qb * tq + lax.broadcasted_iota(i32, (tq, kt), 0)
    kiota = lax.broadcasted_iota(i32, (tq, kt), 1)
    m_ref[...] = jnp.full(m_ref.shape, NEG_INF, f32)
    l_ref[...] = jnp.zeros(l_ref.shape, f32)
    acc_ref[...] = jnp.zeros(acc_ref.shape, f32)

    def tile(k0):
        k = k_ref[0, pl.ds(k0, kt), :].astype(bf16)
        v = v_ref[0, pl.ds(k0, kt), :].astype(bf16)
        kpos = k0 + kiota
        if mode == "slc":
            nf = sel_ref.shape[2] // 2
            er = lax.broadcasted_iota(i32, (2 * nf, kt), 0)
            ec = (k0 + lax.broadcasted_iota(i32, (2 * nf, kt), 1)) // CMP_BLOCK
            sel = sel_ref[0]
            ma = _dot(sel, jnp.where(er == ec, 1.0, 0.0).astype(bf16))
            mb = _dot(sel, jnp.where(er - nf == ec, 1.0, 0.0).astype(bf16))
            causal = kpos <= tpos
            ba = jnp.where(causal & (ma > 0.5), 0.0, NEG_INF)
            bb = jnp.where(causal & (mb > 0.5), 0.0, NEG_INF)
        else:
            ba = jnp.where((kpos <= tpos) & (kpos >= tpos - WINDOW), 0.0, NEG_INF)
            bb = ba
        bias = jnp.concatenate([ba] * 4 + [bb] * 4, axis=0)
        s = _dot_nt(q, k) + bias
        m_old = m_ref[...]
        m_new = jnp.maximum(m_old, jnp.max(s, axis=1, keepdims=True))
        p = jnp.exp(s - m_new)
        alpha = jnp.exp(m_old - m_new)
        l_ref[...] = alpha * l_ref[...] + jnp.sum(p, axis=1, keepdims=True)
        acc_ref[...] = alpha * acc_ref[...] + _dot(p.astype(bf16), v)
        m_ref[...] = m_new

    if mode == "slc":
        def body(t_i, c):
            tile(pl.multiple_of(t_i * kt, kt))
            return c
        lax.fori_loop(0, (qb * tq + tq + kt - 1) // kt, body, 0)
    else:
        tile(pl.multiple_of(jnp.maximum(qb * tq - WINDOW, 0), tq))
    o_ref[0] = _unstack_heads(acc_ref[...] / l_ref[...], tq)


def attn_prompt(qz, kv, sel, mode):
    b, t, _ = qz.shape
    tq = Q_BLOCK
    if mode == "slc":
        kt, kcol, vcol = 512, 4, 6
    else:
        kt, kcol, vcol = WINDOW + tq, 0, 2
        assert t >= kt
    in_specs = [pl.BlockSpec((1, tq, 8 * LANES), lambda b_, kp, i: (b_, i, kp)),
                pl.BlockSpec((1, t, LANES), lambda b_, kp, i: (b_, 0, kcol + kp)),
                pl.BlockSpec((1, t, LANES), lambda b_, kp, i: (b_, 0, vcol + kp))]
    args = [qz, kv, kv]
    if mode == "slc":
        nf2 = sel.shape[2] // 2
        in_specs.append(pl.BlockSpec((1, tq, nf2), lambda b_, kp, i: (b_, i, kp)))
        args.append(sel)
    return pl.pallas_call(
        functools.partial(_attn_prompt_kernel, mode=mode, kt=kt),
        out_shape=jax.ShapeDtypeStruct((b, t, NSA_HEADS * NSA_DH), f32),
        grid=(b, 2, t // tq),
        in_specs=in_specs,
        out_specs=pl.BlockSpec((1, tq, 4 * LANES), lambda b_, kp, i: (b_, i, kp)),
        scratch_shapes=[pltpu.VMEM((8 * tq, 1), f32), pltpu.VMEM((8 * tq, 1), f32), pltpu.VMEM((8 * tq, LANES), f32)],
        compiler_params=_cp(("arbitrary", "arbitrary", "arbitrary")),
        name="attn_" + mode,
    )(*args)


def _cmpsel_sample_kernel(qz_ref, kc_ref, vc_ref, oc_ref, sel_ref, sc_ref, *, q_pos0):
    ts = qz_ref.shape[1]
    nf = kc_ref.shape[1]
    nfp = sc_ref.shape[0]
    rows = 8 * ts
    tpos = q_pos0 + lax.broadcasted_iota(i32, (rows, nf), 0) % ts
    vis = (lax.broadcasted_iota(i32, (rows, nf), 1) + 1) * CMP_BLOCK - 1 <= tpos
    for kp in range(2):
        q = _stack_heads(qz_ref[0, :, kp * 8 * LANES:(kp + 1) * 8 * LANES], ts)
        kc = kc_ref[0, :, kp * LANES:(kp + 1) * LANES]
        vc = vc_ref[0, :, kp * LANES:(kp + 1) * LANES]
        s = jnp.where(vis, _dotp(_dot_nt, q, kc, True), NEG_INF)
        e = jnp.where(vis, jnp.exp(s - jnp.max(s, axis=1, keepdims=True)), 0.0)
        p = e / jnp.maximum(jnp.sum(e, axis=1, keepdims=True), 1e-30)
        oc_ref[0, :, kp * 4 * LANES:(kp + 1) * 4 * LANES] = _unstack_heads(_dotp(_dot, p, vc, True), ts)
        imp_a = p[0:ts] + p[ts:2 * ts] + p[2 * ts:3 * ts] + p[3 * ts:4 * ts]
        imp_b = p[4 * ts:5 * ts] + p[5 * ts:6 * ts] + p[6 * ts:7 * ts] + p[7 * ts:8 * ts]
        imp = jnp.concatenate([imp_a] * 4 + [imp_b] * 4, axis=0)
        if nfp > nf:
            imp = jnp.concatenate([imp, jnp.zeros((rows, nfp - nf), f32)], axis=1)
        imp_t = jnp.concatenate([imp[:, c * LANES:(c + 1) * LANES].T for c in range(nfp // LANES)], axis=0)
        blk = lax.broadcasted_iota(i32, (nfp, rows), 0)
        cur = (q_pos0 + lax.broadcasted_iota(i32, (nfp, rows), 1) % ts) // CMP_BLOCK
        forced = (blk == 0) | (blk == cur) | (blk == cur - 1)
        sc = jnp.where(blk < nf, jnp.where(forced, FORCED_SCORE, imp_t), NEG_INF)
        sc_ref[...] = sc
        rank = _rank_loop(sc_ref, nf)
        sel_ref[0, kp] = jnp.where((rank < TOP_N - 1) & (sc > 0.5 * NEG_INF), 1.0, 0.0)


def cmpsel_sample(qz3, kc, vc, q_pos0):
    nb, ts, _ = qz3.shape
    nf = kc.shape[1]
    nfp = -(-nf // LANES) * LANES
    return pl.pallas_call(
        functools.partial(_cmpsel_sample_kernel, q_pos0=q_pos0),
        out_shape=(jax.ShapeDtypeStruct((nb, ts, NSA_HEADS * NSA_DH), f32),
                   jax.ShapeDtypeStruct((nb, 2, nfp, 8 * ts), f32)),
        grid=(nb,),
        in_specs=[pl.BlockSpec((1, ts, 2 * NSA_HEADS * NSA_DH), lambda b_: (b_, 0, 0)),
                  pl.BlockSpec((1, nf, 256), lambda b_: (b_, 0, 0)),
                  pl.BlockSpec((1, nf, 256), lambda b_: (b_, 0, 0))],
        out_specs=(pl.BlockSpec((1, ts, NSA_HEADS * NSA_DH), lambda b_: (b_, 0, 0)),
                   pl.BlockSpec((1, 2, nfp, 8 * ts), lambda b_: (b_, 0, 0, 0))),
        scratch_shapes=[pltpu.VMEM((nfp, 8 * ts), f32)],
        compiler_params=_cp(("arbitrary",)),
        name="cmpsel_sample",
    )(qz3, kc, vc)


def _slc_sample_kernel(pt_ref, *refs, n_new):
    npg = PAGES_PER_STEP
    kpages, vpages = refs[:npg], refs[npg:2 * npg]
    qz_ref, sel_ref, kn_ref, vn_ref, o_ref, m_ref, l_ref, acc_ref = refs[2 * npg:]
    st = pl.program_id(2)
    ts = qz_ref.shape[1]
    cols = 8 * ts
    q = _stack_heads(qz_ref[0], ts)

    @pl.when(st == 0)
    def _():
        m_ref[...] = jnp.full(m_ref.shape, NEG_INF, f32)
        l_ref[...] = jnp.zeros(l_ref.shape, f32)
        acc_ref[...] = jnp.zeros(acc_ref.shape, f32)

    def update(k, v, bias):
        s = _dotp(_dot_nt, k, q, True) + bias
        m_old = m_ref[...]
        m_new = jnp.maximum(m_old, jnp.max(s, axis=0, keepdims=True))
        p = jnp.exp(s - m_new)
        alpha = jnp.exp(m_old - m_new)
        l_ref[...] = alpha * l_ref[...] + jnp.sum(p, axis=0, keepdims=True)
        acc_ref[...] = alpha * acc_ref[...] + _dotp(_dot_tn, v, p, True)
        m_ref[...] = m_new

    k = jnp.concatenate([r[0, 0] for r in kpages], axis=0)
    v = jnp.concatenate([r[0, 0] for r in vpages], axis=0)
    psz = kpages[0].shape[2]
    bps = npg * psz // CMP_BLOCK
    sel = sel_ref[0, 0]
    er = lax.broadcasted_iota(i32, (bps * CMP_BLOCK, bps), 0) // CMP_BLOCK
    ec = lax.broadcasted_iota(i32, (bps * CMP_BLOCK, bps), 1)
    msk = _dot(jnp.where(er == ec, 1.0, 0.0).astype(bf16), sel.astype(bf16))
    bias = jnp.where(msk > 0.5, 0.0, NEG_INF)
    update(k, v, bias)

    @pl.when(st == pl.num_programs(2) - 1)
    def _():
        kn = kn_ref[0]
        vn = vn_ref[0]
        j = lax.broadcasted_iota(i32, (ts, cols), 0)
        tok = lax.broadcasted_iota(i32, (ts, cols), 1) % ts
        update(kn, vn, jnp.where((j <= tok) & (j < n_new), 0.0, NEG_INF))
        o = (acc_ref[...] / l_ref[...]).T
        o_ref[0] = _unstack_heads(o, ts)


def slc_sample(page_flat, cache4, layer, n_pages, qz3, sel_t, kvc3, n_new):
    nb, ts, _ = qz3.shape
    psz = cache4.shape[2]
    steps = n_pages // PAGES_PER_STEP
    bps = PAGES_PER_STEP * psz // CMP_BLOCK
    cols = 8 * ts

    def page_map(b_, kp, s, pt, kk, col):
        return (layer, pt[b_ * n_pages + s * PAGES_PER_STEP + kk], 0, col + kp)

    kspecs = [pl.BlockSpec((1, 1, psz, LANES), functools.partial(page_map, kk=kk, col=4)) for kk in range(PAGES_PER_STEP)]
    vspecs = [pl.BlockSpec((1, 1, psz, LANES), functools.partial(page_map, kk=kk, col=6)) for kk in range(PAGES_PER_STEP)]
    return pl.pallas_call(
        functools.partial(_slc_sample_kernel, n_new=n_new),
        out_shape=jax.ShapeDtypeStruct((nb, ts, NSA_HEADS * NSA_DH), f32),
        grid_spec=pltpu.PrefetchScalarGridSpec(
            num_scalar_prefetch=1, grid=(nb, 2, steps),
            in_specs=kspecs + vspecs + [
                pl.BlockSpec((1, ts, 8 * LANES), lambda b_, kp, s, pt: (b_, 0, kp)),
                pl.BlockSpec((1, 1, bps, cols), lambda b_, kp, s, pt: (b_, kp, s, 0)),
                pl.BlockSpec((1, ts, LANES), lambda b_, kp, s, pt: (b_, 0, 4 + kp)),
                pl.BlockSpec((1, ts, LANES), lambda b_, kp, s, pt: (b_, 0, 6 + kp))],
            out_specs=pl.BlockSpec((1, ts, 4 * LANES), lambda b_, kp, s, pt: (b_, 0, kp)),
            scratch_shapes=[pltpu.VMEM((1, cols), f32), pltpu.VMEM((1, cols), f32), pltpu.VMEM((LANES, cols), f32)]),
        compiler_params=_cp(("arbitrary", "arbitrary", "arbitrary")),
        name="slc_sample",
    )(page_flat, *([cache4] * (2 * PAGES_PER_STEP)), qz3, sel_t, kvc3, kvc3)


def _win_sample_kernel(qz_ref, old_ref, new_ref, o_ref, nw_ref, *, n_new):
    ts = qz_ref.shape[1]
    wb = old_ref.shape[2]
    rows = 8 * ts
    old = old_ref[0, 0]
    new = new_ref[0]
    tok_o = lax.broadcasted_iota(i32, (rows, wb), 0) % ts
    j_o = lax.broadcasted_iota(i32, (rows, wb), 1)
    bias_o = jnp.where((j_o >= tok_o + (wb - WINDOW)), 0.0, NEG_INF)
    tok_n = lax.broadcasted_iota(i32, (rows, ts), 0) % ts
    j_n = lax.broadcasted_iota(i32, (rows, ts), 1)
    bias_n = jnp.where((j_n <= tok_n) & (j_n < n_new), 0.0, NEG_INF)
    for kp in range(2):
        q = _stack_heads(qz_ref[0, :, kp * 8 * LANES:(kp + 1) * 8 * LANES], ts)
        ko = old[:, kp * LANES:(kp + 1) * LANES]
        vo = old[:, 256 + kp * LANES:256 + (kp + 1) * LANES]
        kn = new[:, kp * LANES:(kp + 1) * LANES]
        vn = new[:, 256 + kp * LANES:256 + (kp + 1) * LANES]
        so = _dotp(_dot_nt, q, ko, True) + bias_o
        sn = _dotp(_dot_nt, q, kn, True) + bias_n
        m = jnp.maximum(jnp.max(so, axis=1, keepdims=True), jnp.max(sn, axis=1, keepdims=True))
        po, pn = jnp.exp(so - m), jnp.exp(sn - m)
        den = jnp.sum(po, axis=1, keepdims=True) + jnp.sum(pn, axis=1, keepdims=True)
        o = (_dotp(_dot, po, vo, True) + _dotp(_dot, pn, vn, True)) / den
        o_ref[0, :, kp * 4 * LANES:(kp + 1) * 4 * LANES] = _unstack_heads(o, ts)
    both = jnp.concatenate([old, new], axis=0)
    nw_ref[0, 0] = pltpu.roll(both, wb + ts - n_new, 0)[0:wb]


def win_sample(qz3, state_win4, layer, kvw3, n_new):
    nb, ts, _ = qz3.shape
    wb = state_win4.shape[2]
    return pl.pallas_call(
        functools.partial(_win_sample_kernel, n_new=n_new),
        out_shape=(jax.ShapeDtypeStruct((nb, ts, NSA_HEADS * NSA_DH), f32),
                   jax.ShapeDtypeStruct((1, nb, wb, 512), f32)),
        grid=(nb,),
        in_specs=[pl.BlockSpec((1, ts, 2 * NSA_HEADS * NSA_DH), lambda b_: (b_, 0, 0)),
                  pl.BlockSpec((1, 1, wb, 512), lambda b_: (layer, b_, 0, 0)),
                  pl.BlockSpec((1, ts, 512), lambda b_: (b_, 0, 0))],
        out_specs=(pl.BlockSpec((1, ts, NSA_HEADS * NSA_DH), lambda b_: (b_, 0, 0)),
                   pl.BlockSpec((1, 1, wb, 512), lambda b_: (0, b_, 0, 0))),
        compiler_params=_cp(("arbitrary",)),
        name="win_sample",
    )(qz3, state_win4, kvw3)


def _mix1_kernel(*refs, nw):
    og_ref, r_ref, ng_ref, mg0_ref, mg1_ref, oc_ref, os_ref, ow_ref, gg_ref = refs[:9]
    wa_refs, wb_refs = refs[9:9 + nw], refs[9 + nw:9 + 2 * nw]
    o_ref = refs[9 + 2 * nw]
    a_refs, b_refs = refs[10 + 2 * nw:10 + 3 * nw], refs[10 + 3 * nw:]
    j = pl.program_id(2)

    @pl.when(j == 0)
    def _():
        ys = []
        for h in range(GLA_HEADS):
            hs = slice(h * GLA_DV, (h + 1) * GLA_DV)
            x = og_ref[0, :, hs]
            y = x * lax.rsqrt(jnp.mean(x * x, axis=-1, keepdims=True) + EPS) * gg_ref[:, hs]
            ys.append(y * _silu(r_ref[0, :, hs]))
        _store_split(jnp.concatenate(ys, axis=1), a_refs)
        ghi, glo = _split2(jax.nn.sigmoid(ng_ref[0]))
        nh = NSA_HEADS * NSA_DH
        er = lax.broadcasted_iota(i32, (LANES, nh), 0)
        ec = lax.broadcasted_iota(i32, (LANES, nh), 1) // NSA_DH
        acc = None
        for br, ref in enumerate((oc_ref, os_ref, ow_ref)):
            ex = jnp.where(er == br * NSA_HEADS + ec, 1.0, 0.0).astype(bf16)
            term = (_dot(ghi, ex) + _dot(glo, ex)) * ref[0]
            acc = term if acc is None else acc + term
        _store_split(acc, b_refs)

    o_ref[0] = (mg0_ref[0] * _mm_split(a_refs, wa_refs) + mg1_ref[0] * _mm_split(b_refs, wb_refs)).astype(o_ref.dtype)


def mix1(o_gla, z, o_cmp, o_slc, o_win, gg, was, wbs):
    grp, t, n1 = o_gla.shape
    d = was[0].shape[1]
    nw = len(was)
    tm, tn = min(t, 512), 1024
    row = lambda w, c: pl.BlockSpec((1, tm, w), functools.partial(lambda g_, i, j, c: (g_, i, c), c=c))
    return pl.pallas_call(
        functools.partial(_mix1_kernel, nw=nw),
        out_shape=jax.ShapeDtypeStruct((grp, t, d), bf16 if nw == 1 else f32),
        grid=(grp, t // tm, d // tn),
        in_specs=[row(n1, 0), row(1024, C_GR // 1024), row(LANES, C_NG // LANES),
                  pl.BlockSpec((1, tm, tn), lambda g_, i, j: (g_, i, C_MG // tn + j)),
                  pl.BlockSpec((1, tm, tn), lambda g_, i, j: (g_, i, (C_MG + d) // tn + j)),
                  row(n1, 0), row(n1, 0), row(n1, 0),
                  pl.BlockSpec((1, n1), lambda g_, i, j: (0, 0))]
        + [pl.BlockSpec((n1, tn), lambda g_, i, j: (0, j))] * (2 * nw),
        out_specs=pl.BlockSpec((1, tm, tn), lambda g_, i, j: (g_, i, j)),
        scratch_shapes=[pltpu.VMEM((tm, n1), bf16)] * (2 * nw),
        compiler_params=_cp(("arbitrary", "arbitrary", "arbitrary")),
        name="mix1",
    )(o_gla, z, z, z, z, o_cmp, o_slc, o_win, gg, *was, *wbs)


def _mm_res_kernel(*refs, nw):
    a_ref = refs[0]
    w_refs = refs[1:1 + nw]
    x_ref, gt_ref, o_ref = refs[1 + nw:]
    o_ref[0] = x_ref[0] + gt_ref[0] * _mm(a_ref[0], [w[...] for w in w_refs])


def mm_res(a, ws, x, gate):
    grp, t, k = a.shape
    d = ws[0].shape[1]
    nw = len(ws)
    tm, tn = min(t, 512), 512
    per_row = gate.shape[1] != 1
    return pl.pallas_call(
        functools.partial(_mm_res_kernel, nw=nw),
        out_shape=jax.ShapeDtypeStruct((grp, t, d), f32),
        grid=(grp, t // tm, d // tn),
        in_specs=[pl.BlockSpec((1, tm, k), lambda g_, i, j: (g_, i, 0))]
        + [pl.BlockSpec((k, tn), lambda g_, i, j: (0, j))] * nw
        + [pl.BlockSpec((1, tm, tn), lambda g_, i, j: (g_, i, j)),
           pl.BlockSpec((1, tm, tn), lambda g_, i, j: (g_, i, j)) if per_row
           else pl.BlockSpec((1, 1, tn), lambda g_, i, j: (g_, 0, j))],
        out_specs=pl.BlockSpec((1, tm, tn), lambda g_, i, j: (g_, i, j)),
        compiler_params=_cp(("arbitrary", "arbitrary", "arbitrary")),
        name="mm_res",
    )(a, *ws, x, gate)


def _ffn1_kernel(*refs, nw):
    x_ref, g_ref, sh_ref, sc_ref = refs[:4]
    wg_refs, wu_refs = refs[4:4 + nw], refs[4 + nw:4 + 2 * nw]
    o_ref = refs[4 + 2 * nw]
    h_refs = refs[5 + 2 * nw:]

    @pl.when(pl.program_id(2) == 0)
    def _():
        _store_split(_modulated(x_ref, g_ref, sh_ref, sc_ref), h_refs)

    o_ref[0] = (_silu(_mm_split(h_refs, wg_refs)) * _mm_split(h_refs, wu_refs)).astype(o_ref.dtype)


def ffn1(x, g, shift, scale, wgs, wus):
    grp, t, d = x.shape
    f = wgs[0].shape[1]
    nw = len(wgs)
    tm, tn = min(t, 512), 512
    per_row = shift.shape[1] != 1
    return pl.pallas_call(
        functools.partial(_ffn1_kernel, nw=nw),
        out_shape=jax.ShapeDtypeStruct((grp, t, f), bf16 if nw == 1 else f32),
        grid=(grp, t // tm, f // tn),
        in_specs=[pl.BlockSpec((1, tm, d), lambda g_, i, j: (g_, i, 0)),
                  pl.BlockSpec((1, d), lambda g_, i, j: (0, 0)),
                  _mod_specs(t, tm, d, per_row), _mod_specs(t, tm, d, per_row)]
        + [pl.BlockSpec((d, tn), lambda g_, i, j: (0, j))] * (2 * nw),
        out_specs=pl.BlockSpec((1, tm, tn), lambda g_, i, j: (g_, i, j)),
        scratch_shapes=[pltpu.VMEM((tm, d), bf16)] * nw,
        compiler_params=_cp(("arbitrary", "arbitrary", "arbitrary")),
        name="ffn1",
    )(x, g, shift, scale, *wgs, *wus)


def _router_kernel(x_ref, g_ref, sh_ref, sc_ref, whi_ref, wlo_ref, h_ref, r_ref):
    h = _modulated(x_ref, g_ref, sh_ref, sc_ref)
    h_ref[0] = h
    hi, lo = _split2(h)
    logits = _dot(hi, whi_ref[...]) + _dot(lo, whi_ref[...]) + _dot(hi, wlo_ref[...])
    lane = lax.broadcasted_iota(i32, logits.shape, 1)
    lg = jnp.where(lane < N_EXPERTS, logits, -jnp.inf)
    v1 = jnp.max(lg, axis=1, keepdims=True)
    i1 = jnp.min(jnp.where(lg == v1, lane, LANES), axis=1, keepdims=True)
    lg2 = jnp.where(lane == i1, -jnp.inf, lg)
    v2 = jnp.max(lg2, axis=1, keepdims=True)
    i2 = jnp.min(jnp.where(lg2 == v2, lane, LANES), axis=1, keepdims=True)
    e2 = jnp.exp(v2 - v1)
    g1 = 1.0 / (1.0 + e2)
    g2 = e2 / (1.0 + e2)
    r_ref[0] = jnp.where(lane == 0, i1.astype(f32), jnp.where(lane == 1, i2.astype(f32),
                         jnp.where(lane == 2, g1, jnp.where(lane == 3, g2, 0.0))))


def router(x, g, shift, scale, whi, wlo):
    grp, t, d = x.shape
    tm = min(t, 512)
    per_row = shift.shape[1] != 1
    mod = (pl.BlockSpec((1, tm, d), lambda g_, i: (g_, i, 0)) if per_row
           else pl.BlockSpec((1, 1, d), lambda g_, i: (g_, 0, 0)))
    return pl.pallas_call(
        _router_kernel,
        out_shape=(jax.ShapeDtypeStruct((grp, t, d), f32), jax.ShapeDtypeStruct((grp, t, LANES), f32)),
        grid=(grp, t // tm),
        in_specs=[pl.BlockSpec((1, tm, d), lambda g_, i: (g_, i, 0)),
                  pl.BlockSpec((1, d), lambda g_, i: (0, 0)), mod, mod,
                  pl.BlockSpec((d, LANES), lambda g_, i: (0, 0)),
                  pl.BlockSpec((d, LANES), lambda g_, i: (0, 0))],
        out_specs=(pl.BlockSpec((1, tm, d), lambda g_, i: (g_, i, 0)),
                   pl.BlockSpec((1, tm, LANES), lambda g_, i: (g_, i, 0))),
        compiler_params=_cp(("arbitrary", "arbitrary")),
        name="router",
    )(x, g, shift, scale, whi, wlo)


def _gather_rows_kernel(idx_ref, src_ref, dst_ref, sem, *, rows):
    base = pl.program_id(0) * rows

    def row_copy(src_row, dst_row):
        return pltpu.make_async_copy(src_ref.at[pl.ds(src_row, 1)], dst_ref.at[pl.ds(dst_row, 1)], sem)

    def start(r, c):
        row_copy(idx_ref[base + r], base + r).start()
        return c

    def wait(r, c):
        row_copy(0, base + r).wait()
        return c

    lax.fori_loop(0, rows, start, 0)
    lax.fori_loop(0, rows, wait, 0)


def gather_rows(idx, src, n_out):
    rows = 256 if n_out % 256 == 0 else 128
    return pl.pallas_call(
        functools.partial(_gather_rows_kernel, rows=rows),
        out_shape=jax.ShapeDtypeStruct((n_out, src.shape[1]), src.dtype),
        grid_spec=pltpu.PrefetchScalarGridSpec(
            num_scalar_prefetch=1, grid=(n_out // rows,),
            in_specs=[pl.BlockSpec(memory_space=pl.ANY)],
            out_specs=pl.BlockSpec(memory_space=pl.ANY),
            scratch_shapes=[pltpu.SemaphoreType.DMA]),
        compiler_params=pltpu.CompilerParams(dimension_semantics=("arbitrary",)),
        name="gather_rows",
    )(idx, src)


def _moe1_kernel(te_ref, nt_ref, xs_ref, *refs, nw):
    wg_refs, wu_refs, o_ref = refs[:nw], refs[nw:2 * nw], refs[2 * nw]

    @pl.when(pl.program_id(1) < nt_ref[0])
    def _():
        x = xs_ref[...]
        gate = _mm(x, [w[0] for w in wg_refs])
        o_ref[...] = (_silu(gate) * _mm(x, [w[0] for w in wu_refs])).astype(o_ref.dtype)

    @pl.when(pl.program_id(1) >= nt_ref[0])
    def _():
        o_ref[...] = jnp.zeros(o_ref.shape, o_ref.dtype)


def moe1(tile_expert, n_tiles_used, xs, wgs, wus, tm):
    p, d = xs.shape
    fe = wgs[0].shape[2]
    nw = len(wgs)
    tn = fe // 2 if nw == 1 else 256
    return pl.pallas_call(
        functools.partial(_moe1_kernel, nw=nw),
        out_shape=jax.ShapeDtypeStruct((p, fe), bf16 if nw == 1 else f32),
        grid_spec=pltpu.PrefetchScalarGridSpec(
            num_scalar_prefetch=2, grid=(fe // tn, p // tm),
            in_specs=[pl.BlockSpec((tm, d), lambda j, i, te, nt: (i, 0))]
            + [pl.BlockSpec((1, d, tn), lambda j, i, te, nt: (te[i], 0, j))] * (2 * nw),
            out_specs=pl.BlockSpec((tm, tn), lambda j, i, te, nt: (i, j))),
        compiler_params=_cp(("arbitrary", "arbitrary")),
        name="moe1",
    )(tile_expert, n_tiles_used, xs, *wgs, *wus)


def _moe2_kernel(te_ref, nt_ref, a_ref, *refs, nw):
    wd_refs, sg_ref, o_ref = refs[:nw], refs[nw], refs[nw + 1]

    @pl.when(pl.program_id(1) < nt_ref[0])
    def _():
        o_ref[...] = sg_ref[...] * _mm(a_ref[...], [w[0] for w in wd_refs])

    @pl.when(pl.program_id(1) >= nt_ref[0])
    def _():
        o_ref[...] = jnp.zeros(o_ref.shape, o_ref.dtype)


def moe2(tile_expert, n_tiles_used, act, wds, slot_gate, tm):
    p, fe = act.shape
    d = wds[0].shape[2]
    nw = len(wds)
    tn = d // 2 if nw == 1 else 512
    return pl.pallas_call(
        functools.partial(_moe2_kernel, nw=nw),
        out_shape=jax.ShapeDtypeStruct((p, d), f32),
        grid_spec=pltpu.PrefetchScalarGridSpec(
            num_scalar_prefetch=2, grid=(d // tn, p // tm),
            in_specs=[pl.BlockSpec((tm, fe), lambda j, i, te, nt: (i, 0))]
            + [pl.BlockSpec((1, fe, tn), lambda j, i, te, nt: (te[i], 0, j))] * nw
            + [pl.BlockSpec((tm, 1), lambda j, i, te, nt: (i, 0))],
            out_specs=pl.BlockSpec((tm, tn), lambda j, i, te, nt: (i, j))),
        compiler_params=_cp(("arbitrary", "arbitrary")),
        name="moe2",
    )(tile_expert, n_tiles_used, act, *wds, slot_gate)


def _combine_kernel(x_ref, gt_ref, y1_ref, y2_ref, o_ref):
    o_ref[0] = x_ref[0] + gt_ref[0] * (y1_ref[0] + y2_ref[0])


def combine(x, gate, y1, y2):
    grp, t, d = x.shape
    tm = min(t, 512)
    per_row = gate.shape[1] != 1
    row = pl.BlockSpec((1, tm, d), lambda g_, i: (g_, i, 0))
    return pl.pallas_call(
        _combine_kernel,
        out_shape=jax.ShapeDtypeStruct((grp, t, d), f32),
        grid=(grp, t // tm),
        in_specs=[row, row if per_row else pl.BlockSpec((1, 1, d), lambda g_, i: (g_, 0, 0)), row, row],
        out_specs=row,
        compiler_params=_cp(("arbitrary", "arbitrary")),
        name="combine",
    )(x, gate, y1, y2)


def moe_layer(x, g, shift, scale, gate, whi, wlo, wgs, wus, wds):
    grp, t, d = x.shape
    n = grp * t
    tm = 512 if n >= 4096 else 128
    h, route = router(x, g, shift, scale, whi, wlo)
    route = route.reshape(n, LANES)
    e_idx = route[:, 0:TOP_K].astype(i32)
    gates = route[:, 2:2 + TOP_K]
    e_flat = e_idx.reshape(-1)
    onehot = (e_flat[:, None] == jnp.arange(N_EXPERTS, dtype=i32)[None, :]).astype(i32)
    within = jnp.cumsum(onehot, axis=0) - onehot
    counts = jnp.sum(onehot, axis=0)
    padded = (counts + tm - 1) // tm * tm
    starts = jnp.cumsum(padded) - padded
    slot = jnp.sum(onehot * (starts[None, :] + within), axis=1)
    p = TOP_K * n + N_EXPERTS * tm
    p = -(-p // 256) * 256
    slot_tok = jnp.zeros((p,), i32).at[slot].set(jnp.arange(TOP_K * n, dtype=i32) // TOP_K)
    slot_gate = jnp.zeros((p,), f32).at[slot].set(gates.reshape(-1))
    n_tiles = p // tm
    ends = jnp.cumsum(padded)
    tile_start = jnp.arange(n_tiles, dtype=i32) * tm
    tile_expert = jnp.minimum(jnp.sum((tile_start[:, None] >= ends[None, :]).astype(i32), axis=1), N_EXPERTS - 1).astype(i32)
    n_used = (ends[-1] // tm).astype(i32).reshape(1)
    xs = gather_rows(slot_tok, h.reshape(n, d), p)
    act = moe1(tile_expert, n_used, xs, wgs, wus, tm)
    ys = moe2(tile_expert, n_used, act, wds, slot_gate.reshape(p, 1), tm)
    slots = slot.reshape(n, TOP_K)
    y1 = gather_rows(slots[:, 0], ys, n).reshape(grp, t, d)
    y2 = gather_rows(slots[:, 1], ys, n).reshape(grp, t, d)
    return combine(x, gate, y1, y2)


def _rope_tables(pos):
    half = ROPE_DIMS // 2
    inv = ROPE_THETA ** (-(jnp.arange(half, dtype=f32) * 2.0 / ROPE_DIMS))
    ang = pos.astype(f32)[:, None] * inv[None, :]
    cos, sin = jnp.cos(ang), jnp.sin(ang)
    n = pos.shape[0]
    pad = jnp.zeros((n, NSA_DH - ROPE_DIMS), f32)
    zero = jnp.zeros((n, half), f32)
    c64 = jnp.concatenate([cos, cos, pad + 1.0], axis=1)
    a64 = jnp.concatenate([-sin, zero, pad], axis=1)
    b64 = jnp.concatenate([zero, sin, pad], axis=1)
    return tuple(jnp.tile(x, (1, 2)) for x in (c64, a64, b64))


def _hilo(w):
    hi = lax.reduce_precision(w, exponent_bits=8, mantissa_bits=7)
    return hi.astype(bf16), (w - hi).astype(bf16)


def _pair_diag(w):
    z = jnp.zeros_like(w)
    return jnp.concatenate([jnp.concatenate([w, z], axis=-1), jnp.concatenate([z, w], axis=-1)], axis=-2)


def _mods(mod_l, rows, per_row_t):
    d = mod_l.shape[1] // 6
    m = mod_l.reshape(rows, 2, 3, d)
    out = {}
    for si, sname in enumerate(("mix", "ffn")):
        for ti, tname in enumerate(("shift", "scale", "gate")):
            v = m[:, si, ti]
            if per_row_t:
                v = jnp.repeat(v, per_row_t, axis=0)[None]
            else:
                v = v[:, None, :]
            out[sname + "_" + tname] = v
    return out


def kernel(x_prompt, x_sample, cache_kv, state_kv_win, state_gla, page_table, c_prompt, c_sample, norm_g, w_ada, b_ada, w_in, w_alpha2, b_alpha, gla_norm_g, qk_g, phi_pe, phi_w1, phi_w2, w_branch_a, w_branch_b, w_out, w_ffn_gate, w_ffn_up, w_ffn_down, w_router, w_exp_gate, w_exp_up, w_exp_down):
    bp, tp, d = x_prompt.shape
    bs, t_new, _ = x_sample.shape
    depth = w_in.shape[0]
    n_pool, psz = cache_kv.shape[1], cache_kv.shape[2]
    n_pages = page_table.shape[1]
    past_len = n_pages * psz
    wbuf = state_kv_win.shape[2]
    ts = TS_PAD

    w_in_r = _hilo(jnp.concatenate([
        w_in[:, :, 0:2048], w_in[:, :, 2064:3088], w_in[:, :, 3088:4112], w_in[:, :, 4112:5648],
        w_in[:, :, 5648:5696], w_in[:, :, 2048:2064], jnp.zeros((depth, d, C_MG - C_GA - GLA_RANK), f32),
        w_in[:, :, 5696:9792]], axis=2))
    ba2 = b_alpha.reshape(depth, 1, -1)
    wba, wbb, wo = _hilo(w_branch_a), _hilo(w_branch_b), _hilo(w_out)
    wfg, wfu, wfd = _hilo(w_ffn_gate), _hilo(w_ffn_up), _hilo(w_ffn_down)
    weg, weu, wed = _hilo(w_exp_gate), _hilo(w_exp_up), _hilo(w_exp_down)
    wr_hi, wr_lo = _hilo(jnp.pad(w_router, ((0, 0), (0, 0), (0, LANES - N_EXPERTS))))
    w1c = _hilo(_pair_diag(phi_w1.reshape(depth, 2, CMP_BLOCK, NSA_DH, -1)))
    w2c = _hilo(_pair_diag(phi_w2))
    pe2 = jnp.tile(phi_pe, (1, 1, 1, 2))
    gq = jnp.tile(qk_g[:, 0:1, :], (1, 1, 2))
    gk = jnp.tile(qk_g[:, 1:4, :], (1, 1, 2))
    gkc = jnp.tile(qk_g[:, 4:5, :], (1, 1, 2))
    ggl = gla_norm_g.reshape(depth, 1, -1)
    ng = norm_g.reshape(depth, 2, 1, d)

    c_pad = jnp.zeros((16, d), f32).at[:bp].set(c_prompt).at[bp:bp + bs].set(c_sample)
    mod_all = ada_all(c_pad, w_ada, b_ada)

    cos_p, sa_p, sb_p = _rope_tables(jnp.arange(tp, dtype=i32))
    pos_s = past_len + (jnp.arange(bs * ts, dtype=i32) % ts)
    cos_s, sa_s, sb_s = _rope_tables(pos_s)

    page_flat = page_table.reshape(-1).astype(i32)
    cache4 = cache_kv.reshape(depth, n_pool, psz, 4 * NSA_KV_HEADS * NSA_DH)
    win4 = state_kv_win.reshape(depth, bs, wbuf, 2 * NSA_KV_HEADS * NSA_DH)

    x_p = x_prompt
    x_s = jnp.zeros((bs, ts, d), f32).at[:, :t_new].set(x_sample).reshape(1, bs * ts, d)
    kv_p, win_p, gla_p, kv_s, win_s, gla_s = [], [], [], [], [], []
    hi = lambda w, i: (w[0][i],)
    hl = lambda w, i: (w[0][i], w[1][i])
    flat = lambda a: a.reshape(1, bs * ts, -1)
    for l in range(depth):
        mp = _mods(mod_all[l, :bp], bp, 0)
        ms = _mods(mod_all[l, bp:bp + bs], bs, ts)
        z = proj(x_p, ng[l, 0], mp["mix_shift"], mp["mix_scale"], hi(w_in_r, l))
        qz, kvc, kvw = nsa_prep(z, cos_p, sa_p, sb_p, gq[l], gk[l], bf16)
        o_gla, st = gla(z, w_alpha2[l], ba2[l], None, GLA_CHUNK, GLA_SUB, GLA_CHUNK, False)
        kc, vc = compress_prompt(kvc, pe2[l], w1c[0][l], w2c[0][l], gkc[l])
        o_cmp, sel = cmpsel_prompt(qz, kc, vc)
        o_slc = attn_prompt(qz, kvc, sel, "slc")
        o_win = attn_prompt(qz, kvw, None, "win")
        merged = mix1(o_gla, z, o_cmp, o_slc, o_win, ggl[l], hi(wba, l), hi(wbb, l))
        x_p = mm_res(merged, hi(wo, l), x_p, mp["mix_gate"])
        kv_p.append(kvc.reshape(bp, tp, 4, NSA_KV_HEADS, NSA_DH))
        win_p.append(kvw[:, tp - WINDOW:].reshape(bp, WINDOW, 2, NSA_KV_HEADS, NSA_DH))
        gla_p.append(st)
        z = proj(x_s, ng[l, 0], ms["mix_shift"], ms["mix_scale"], hl(w_in_r, l))
        qz, kvc, kvw = nsa_prep(z, cos_s, sa_s, sb_s, gq[l], gk[l], f32)
        z3 = z.reshape(bs, ts, N_PROJ)
        o_gla, st = gla(z3, w_alpha2[l], ba2[l], state_gla[l], ts, ts, t_new, True)
        qz3 = qz.reshape(bs, ts, -1)
        kvc3 = kvc.reshape(bs, ts, -1)
        kvw3 = kvw.reshape(bs, ts, -1)
        kc, vc = compress_sample(page_flat, cache4, l, bs, n_pages, pe2[l], hl(w1c, l), hl(w2c, l), gkc[l])
        o_cmp, sel_t = cmpsel_sample(qz3, kc, vc, past_len)
        o_slc = slc_sample(page_flat, cache4, l, n_pages, qz3, sel_t, kvc3, t_new)
        o_win, new_win = win_sample(qz3, win4, l, kvw3, t_new)
        merged = mix1(flat(o_gla), z, flat(o_cmp), flat(o_slc), flat(o_win), ggl[l], hl(wba, l), hl(wbb, l))
        x_s = mm_res(merged, hl(wo, l), x_s, ms["mix_gate"])
        kv_s.append(kvc3[:, :t_new].reshape(bs, t_new, 4, NSA_KV_HEADS, NSA_DH))
        win_s.append(new_win.reshape(bs, wbuf, 2, NSA_KV_HEADS, NSA_DH))
        gla_s.append(st)
        e = l // 2
        if l % 2 == 0:
            a = ffn1(x_p, ng[l, 1], mp["ffn_shift"], mp["ffn_scale"], hi(wfg, e), hi(wfu, e))
            x_p = mm_res(a, hi(wfd, e), x_p, mp["ffn_gate"])
            a = ffn1(x_s, ng[l, 1], ms["ffn_shift"], ms["ffn_scale"], hl(wfg, e), hl(wfu, e))
            x_s = mm_res(a, hl(wfd, e), x_s, ms["ffn_gate"])
        else:
            x_p = moe_layer(x_p, ng[l, 1], mp["ffn_shift"], mp["ffn_scale"], mp["ffn_gate"],
                            wr_hi[e], wr_lo[e], hi(weg, e), hi(weu, e), hi(wed, e))
            x_s = moe_layer(x_s, ng[l, 1], ms["ffn_shift"], ms["ffn_scale"], ms["ffn_gate"],
                            wr_hi[e], wr_lo[e], hl(weg, e), hl(weu, e), hl(wed, e))
    y_s = x_s.reshape(bs, ts, d)[:, :t_new]
    return (x_p, y_s, jnp.stack(kv_p), jnp.stack(win_p), jnp.stack(gla_p),
            jnp.stack(kv_s), jnp.stack(win_s), jnp.stack(gla_s))
```

```python
import functools

import jax
import jax.numpy as jnp
from jax import lax
from jax.experimental import pallas as pl
from jax.experimental.pallas import tpu as pltpu

f32 = jnp.float32
bf16 = jnp.bfloat16
i32 = jnp.int32

GLA_HEADS = 4
GLA_DK = 128
GLA_DV = 256
GLA_RANK = 16
GLA_TAU = 16.0
GLA_CHUNK = 64
GLA_SUB = 16
NSA_HEADS = 16
NSA_KV_HEADS = 4
NSA_DH = 64
CMP_BLOCK = 64
TOP_N = 16
WINDOW = 512
Q_BLOCK = 128
ROPE_THETA = 500000.0
ROPE_DIMS = NSA_DH // 4
N_EXPERTS = 8
TOP_K = 2
EPS = 1e-6
NEG_INF = -1e30
FORCED_SCORE = 1e4

LANES = 128
TS_PAD = 16
VMEM_LIMIT = 56 * 1024 * 1024

C_GQ, C_GK, C_GV, C_GR, C_NQ, C_KVC, C_KVW, C_NG, C_GA, C_MG = 0, 512, 1024, 2048, 3072, 4096, 5120, 5632, 5680, 6144
N_PROJ = 10240
PAGES_PER_STEP = 8


def _cp(sem):
    return pltpu.CompilerParams(dimension_semantics=sem, vmem_limit_bytes=VMEM_LIMIT)


def _dot(a, b):
    return jnp.dot(a, b, preferred_element_type=f32)


def _dot_nt(a, b):
    return lax.dot_general(a, b, (((1,), (1,)), ((), ())), preferred_element_type=f32)


def _dot_tn(a, b):
    return lax.dot_general(a, b, (((0,), (0,)), ((), ())), preferred_element_type=f32)


def _split2(x):
    hi = x.astype(bf16)
    lo = (x - hi.astype(f32)).astype(bf16)
    return hi, lo


def _split3(x):
    hi = x.astype(bf16)
    r = x - hi.astype(f32)
    mid = r.astype(bf16)
    lo = (r - mid.astype(f32)).astype(bf16)
    return hi, mid, lo


def _dotp(fn, a, b, hp):
    if not hp:
        return fn(a.astype(bf16), b.astype(bf16))
    ah, al = _split2(a)
    bh, bl = _split2(b)
    return fn(ah, bh) + fn(al, bh) + fn(ah, bl)


def _mm(a, ws):
    if len(ws) == 1:
        return _dot(a.astype(bf16), ws[0])
    ah, al = _split2(a)
    return _dot(ah, ws[0]) + _dot(al, ws[0]) + _dot(ah, ws[1])


def _silu(x):
    return x * jax.nn.sigmoid(x)


def _group_ones(n, group):
    r = lax.broadcasted_iota(i32, (n, n), 0) // group
    c = lax.broadcasted_iota(i32, (n, n), 1) // group
    return jnp.where(r == c, 1.0, 0.0).astype(bf16)


def _rms64(x, gain):
    e = _group_ones(LANES, NSA_DH)
    hi, lo = _split2(x * x)
    ssq = _dot(hi, e) + _dot(lo, e)
    return x * lax.rsqrt(ssq * (1.0 / NSA_DH) + EPS) * gain


def _modulated(x_ref, g_ref, sh_ref, sc_ref):
    x = x_ref[0]
    y = x * lax.rsqrt(jnp.mean(x * x, axis=-1, keepdims=True) + EPS) * g_ref[...]
    return y * (1.0 + sc_ref[0]) + sh_ref[0]


def _mod_specs(t, tm, d, per_row):
    if per_row:
        return pl.BlockSpec((1, tm, d), lambda g, i, j: (g, i, 0))
    return pl.BlockSpec((1, 1, d), lambda g, i, j: (g, 0, 0))


def _ada_kernel(c_ref, w_ref, b_ref, o_ref):
    o_ref[0] = _dotp(_dot, c_ref[...], w_ref[0], True) + b_ref[0]


def ada_all(c_pad, w_ada, b_ada):
    nl, d, n = w_ada.shape
    tn = 1024
    return pl.pallas_call(
        _ada_kernel,
        out_shape=jax.ShapeDtypeStruct((nl, c_pad.shape[0], n), f32),
        grid=(nl, n // tn),
        in_specs=[pl.BlockSpec(c_pad.shape, lambda l, j: (0, 0)),
                  pl.BlockSpec((1, d, tn), lambda l, j: (l, 0, j)),
                  pl.BlockSpec((1, 1, tn), lambda l, j: (l, 0, j))],
        out_specs=pl.BlockSpec((1, c_pad.shape[0], tn), lambda l, j: (l, 0, j)),
        compiler_params=_cp(("arbitrary", "arbitrary")),
        name="ada",
    )(c_pad, w_ada, b_ada.reshape(nl, 1, n))


def _store_split(h, h_refs):
    if len(h_refs) == 1:
        h_refs[0][...] = h.astype(bf16)
    else:
        hi, lo = _split2(h)
        h_refs[0][...] = hi
        h_refs[1][...] = lo


def _mm_split(h_refs, w_refs):
    acc = _dot(h_refs[0][...], w_refs[0][...])
    if len(w_refs) == 2:
        acc = acc + _dot(h_refs[1][...], w_refs[0][...]) + _dot(h_refs[0][...], w_refs[1][...])
    return acc


def _proj_kernel(*refs, sig_from, nw):
    x_ref, g_ref, sh_ref, sc_ref = refs[:4]
    w_refs = refs[4:4 + nw]
    o_ref = refs[4 + nw]
    h_refs = refs[5 + nw:]
    j = pl.program_id(2)

    @pl.when(j == 0)
    def _():
        _store_split(_modulated(x_ref, g_ref, sh_ref, sc_ref), h_refs)

    acc = _mm_split(h_refs, w_refs)

    @pl.when(j < sig_from)
    def _():
        o_ref[0] = acc

    @pl.when(j >= sig_from)
    def _():
        o_ref[0] = jax.nn.sigmoid(acc)


def proj(x, g, shift, scale, ws):
    grp, t, d = x.shape
    n = ws[0].shape[1]
    tm, tn = min(t, 512), 1024
    per_row = shift.shape[1] != 1
    nw = len(ws)
    return pl.pallas_call(
        functools.partial(_proj_kernel, sig_from=C_MG // tn, nw=nw),
        out_shape=jax.ShapeDtypeStruct((grp, t, n), f32),
        grid=(grp, t // tm, n // tn),
        in_specs=[pl.BlockSpec((1, tm, d), lambda g_, i, j: (g_, i, 0)),
                  pl.BlockSpec((1, d), lambda g_, i, j: (0, 0)),
                  _mod_specs(t, tm, d, per_row), _mod_specs(t, tm, d, per_row)]
        + [pl.BlockSpec((d, tn), lambda g_, i, j: (0, j))] * nw,
        out_specs=pl.BlockSpec((1, tm, tn), lambda g_, i, j: (g_, i, j)),
        scratch_shapes=[pltpu.VMEM((tm, d), bf16)] * nw,
        compiler_params=_cp(("arbitrary", "arbitrary", "arbitrary")),
        name="proj",
    )(x, g, shift, scale, *ws)


def _rope(y, cos, sa, sb):
    return y * cos + pltpu.roll(y, LANES - ROPE_DIMS // 2, 1) * sa + pltpu.roll(y, ROPE_DIMS // 2, 1) * sb


def _nsa_prep_kernel(nq_ref, kvc_ref, kvw_ref, cos_ref, sa_ref, sb_ref, gq_ref, gk_ref, qz_ref, oc_ref, ow_ref):
    cos, sa, sb = cos_ref[...], sa_ref[...], sb_ref[...]
    lane = lax.broadcasted_iota(i32, cos.shape, 1)
    left = lane < NSA_DH
    for m in range(NSA_HEADS // 2):
        y = _rope(_rms64(nq_ref[0, :, m * LANES:(m + 1) * LANES], gq_ref[...]), cos, sa, sb) * (NSA_DH ** -0.5)
        yr = pltpu.roll(y, NSA_DH, 1)
        if (m // 2) % 2 == 0:
            a, b = jnp.where(left, y, 0.0), jnp.where(left, yr, 0.0)
        else:
            a, b = jnp.where(left, 0.0, yr), jnp.where(left, 0.0, y)
        qz_ref[0, :, (2 * m) * LANES:(2 * m + 1) * LANES] = a.astype(qz_ref.dtype)
        qz_ref[0, :, (2 * m + 1) * LANES:(2 * m + 2) * LANES] = b.astype(qz_ref.dtype)
    for c in range(8):
        x = kvc_ref[0, :, c * LANES:(c + 1) * LANES]
        if (c // 2) % 2 == 0:
            x = _rope(_rms64(x, gk_ref[c // 4:c // 4 + 1, :]), cos, sa, sb)
        oc_ref[0, :, c * LANES:(c + 1) * LANES] = x
    for c in range(4):
        x = kvw_ref[0, :, c * LANES:(c + 1) * LANES]
        if c < 2:
            x = _rope(_rms64(x, gk_ref[2:3, :]), cos, sa, sb)
        ow_ref[0, :, c * LANES:(c + 1) * LANES] = x


def nsa_prep(z, cos, sa, sb, gq, gk, q_dtype):
    grp, t, _ = z.shape
    tm = min(t, 256)
    tab = pl.BlockSpec((tm, LANES), lambda g_, i: (i, 0))
    return pl.pallas_call(
        _nsa_prep_kernel,
        out_shape=(jax.ShapeDtypeStruct((grp, t, 2 * NSA_HEADS * NSA_DH), q_dtype),
                   jax.ShapeDtypeStruct((grp, t, 1024), f32),
                   jax.ShapeDtypeStruct((grp, t, 512), f32)),
        grid=(grp, t // tm),
        in_specs=[pl.BlockSpec((1, tm, 1024), lambda g_, i: (g_, i, C_NQ // 1024)),
                  pl.BlockSpec((1, tm, 1024), lambda g_, i: (g_, i, C_KVC // 1024)),
                  pl.BlockSpec((1, tm, 512), lambda g_, i: (g_, i, C_KVW // 512)),
                  tab, tab, tab,
                  pl.BlockSpec((1, LANES), lambda g_, i: (0, 0)),
                  pl.BlockSpec((3, LANES), lambda g_, i: (0, 0))],
        out_specs=(pl.BlockSpec((1, tm, 2 * NSA_HEADS * NSA_DH), lambda g_, i: (g_, i, 0)),
                   pl.BlockSpec((1, tm, 1024), lambda g_, i: (g_, i, 0)),
                   pl.BlockSpec((1, tm, 512), lambda g_, i: (g_, i, 0))),
        compiler_params=_cp(("arbitrary", "arbitrary")),
        name="nsa_prep",
    )(z, z, z, cos, sa, sb, gq, gk)


def _gla_kernel(*refs, chunk, sub, n_valid, has_init, hp):
    if has_init:
        q_ref, k_ref, v_ref, ga_ref, wa_ref, ba_ref, s0_ref, o_ref, so_ref, st_ref = refs
    else:
        q_ref, k_ref, v_ref, ga_ref, wa_ref, ba_ref, o_ref, so_ref, st_ref = refs
    i = pl.program_id(1)
    tb = q_ref.shape[1]
    ns = chunk // sub

    @pl.when(i == 0)
    def _():
        for h in range(GLA_HEADS):
            if has_init:
                st_ref[h] = s0_ref[0, h].T
            else:
                st_ref[h] = jnp.zeros((GLA_DV, GLA_DK), f32)

    ri = lax.broadcasted_iota(i32, (chunk, chunk), 0)
    ci = lax.broadcasted_iota(i32, (chunk, chunk), 1)
    ci_sub = lax.broadcasted_iota(i32, (sub, chunk), 1)
    tri =jnp.where(ri >= ci, 1.0, 0.0).astype(bf16)
    rrow = lax.broadcasted_iota(i32, (chunk, GLA_DK), 0)
    rmod = rrow % sub
    ga_off = C_GA % LANES

    def do_chunk(c, carry):
        r0 = pl.multiple_of(c * chunk, chunk)
        ga = ga_ref[0, pl.ds(r0, chunk), :][:, ga_off:ga_off + GLA_RANK]
        for h in range(GLA_HEADS):
            hs = slice(h * GLA_DK, (h + 1) * GLA_DK)
            q = q_ref[0, pl.ds(r0, chunk), hs] * (GLA_DK ** -0.5)
            k = k_ref[0, pl.ds(r0, chunk), hs]
            v = v_ref[0, pl.ds(r0, chunk), h * GLA_DV:(h + 1) * GLA_DV]
            x = _dotp(_dot, ga, wa_ref[:, hs], hp) + ba_ref[:, hs]
            la = (jnp.minimum(x, 0.0) - jnp.log1p(jnp.exp(-jnp.abs(x)))) * (1.0 / GLA_TAU)
            if n_valid < chunk:
                la = jnp.where(rrow < n_valid, la, 0.0)
                k = jnp.where(rrow < n_valid, k, 0.0)
            l1, l2, l3 = _split3(la)
            cum = _dot(tri, l1) + _dot(tri, l2) + _dot(tri, l3)
            parts = [jnp.zeros((sub, chunk), f32)]
            for sb_i in range(1, ns):
                lo = sb_i * sub
                m_i = cum[lo:lo + 1, :]
                q_i = q[lo:lo + sub] * jnp.exp(cum[lo:lo + sub] - m_i)
                k_i = k * jnp.exp(jnp.minimum(m_i - cum, 0.0))
                parts.append(jnp.where(ci_sub < lo, _dotp(_dot_nt, q_i, k_i, hp), 0.0))
            scores = parts[0] if ns == 1 else jnp.concatenate(parts, axis=0)
            for j in range(min(sub, n_valid)):
                kj = [jnp.broadcast_to(k[s * sub + j:s * sub + j + 1, :], (sub, GLA_DK)) for s in range(ns)]
                cj = [jnp.broadcast_to(cum[s * sub + j:s * sub + j + 1, :], (sub, GLA_DK)) for s in range(ns)]
                kj = kj[0] if ns == 1 else jnp.concatenate(kj, axis=0)
                cj = cj[0] if ns == 1 else jnp.concatenate(cj, axis=0)
                e = jnp.exp(jnp.where(rmod >= j, cum - cj, NEG_INF))
                val = jnp.sum(q * kj * e, axis=1, keepdims=True)
                scores = jnp.where(ci == (ri // sub) * sub + j, val, scores)
            st = st_ref[h]
            o = _dotp(_dot, scores, v, hp) + _dotp(_dot_nt, q * jnp.exp(cum), st, hp)
            o_ref[0, pl.ds(r0, chunk), h * GLA_DV:(h + 1) * GLA_DV] = o
            last = cum[chunk - 1:chunk, :]
            st_ref[h] = st * jnp.exp(last) + _dotp(_dot_tn, v, k * jnp.exp(last - cum), hp)
        return carry

    lax.fori_loop(0, tb // chunk, do_chunk, 0)

    @pl.when(i == pl.num_programs(1) - 1)
    def _():
        for h in range(GLA_HEADS):
            so_ref[0, h] = st_ref[h].T


def gla(z, wa, ba, s0, chunk, sub, n_valid, hp):
    b, t, _ = z.shape
    tb = min(t, 256)
    has_init = s0 is not None
    hk = GLA_HEADS * GLA_DK
    in_specs = [pl.BlockSpec((1, tb, hk), lambda b_, i: (b_, i, C_GQ // hk)),
                pl.BlockSpec((1, tb, hk), lambda b_, i: (b_, i, C_GK // hk)),
                pl.BlockSpec((1, tb, GLA_HEADS * GLA_DV), lambda b_, i: (b_, i, C_GV // (GLA_HEADS * GLA_DV))),
                pl.BlockSpec((1, tb, LANES), lambda b_, i: (b_, i, C_GA // LANES)),
                pl.BlockSpec((GLA_RANK, hk), lambda b_, i: (0, 0)),
                pl.BlockSpec((1, hk), lambda b_, i: (0, 0))]
    args = [z, z, z, z, wa, ba]
    if has_init:
        in_specs.append(pl.BlockSpec((1, GLA_HEADS, GLA_DK, GLA_DV), lambda b_, i: (b_, 0, 0, 0)))
        args.append(s0)
    return pl.pallas_call(
        functools.partial(_gla_kernel, chunk=chunk, sub=sub, n_valid=n_valid, has_init=has_init, hp=hp),
        out_shape=(jax.ShapeDtypeStruct((b, t, GLA_HEADS * GLA_DV), f32),
                   jax.ShapeDtypeStruct((b, GLA_HEADS, GLA_DK, GLA_DV), f32)),
        grid=(b, t // tb),
        in_specs=in_specs,
        out_specs=(pl.BlockSpec((1, tb, GLA_HEADS * GLA_DV), lambda b_, i: (b_, i, 0)),
                   pl.BlockSpec((1, GLA_HEADS, GLA_DK, GLA_DV), lambda b_, i: (b_, 0, 0, 0))),
        scratch_shapes=[pltpu.VMEM((GLA_HEADS, GLA_DV, GLA_DK), f32)],
        compiler_params=_cp(("arbitrary", "arbitrary")),
        name="gla",
    )(*args)


def _compress_core(x_refs, pe_ref, w1_refs, w2_refs, gk_ref, kc_ref, vc_ref, nblk):
    for c in range(4):
        br = c // 2

        def body(s, acc, c=c, br=br):
            rows = x_refs[c][pl.ds(s, nblk, stride=CMP_BLOCK), :] + pe_ref[br, pl.ds(s, 1), :]
            return acc + _mm(rows, [w[br, s] for w in w1_refs])

        acc = lax.fori_loop(0, CMP_BLOCK, body, jnp.zeros((nblk, 2 * LANES), f32))
        y = _mm(_silu(acc), [w[br] for w in w2_refs])
        if br == 0:
            kc_ref[0, :, (c % 2) * LANES:(c % 2 + 1) * LANES] = _rms64(y, gk_ref[...])
        else:
            vc_ref[0, :, (c % 2) * LANES:(c % 2 + 1) * LANES] = y


def _compress_prompt_kernel(x0, x1, x2, x3, pe_ref, w1_ref, w2_ref, gk_ref, kc_ref, vc_ref):
    nblk = x0.shape[1] // CMP_BLOCK
    _compress_core([x0.at[0], x1.at[0], x2.at[0], x3.at[0]], pe_ref, (w1_ref,), (w2_ref,), gk_ref, kc_ref, vc_ref, nblk)


def compress_prompt(kvc, pe2, w1c, w2c, gkc):
    b, t, _ = kvc.shape
    nblk = t // CMP_BLOCK
    cst = lambda shape: pl.BlockSpec(shape, lambda b_: (0,) * len(shape))
    xs = [pl.BlockSpec((1, t, LANES), functools.partial(lambda b_, c: (b_, 0, c), c=c)) for c in range(4)]
    return pl.pallas_call(
        _compress_prompt_kernel,
        out_shape=(jax.ShapeDtypeStruct((b, nblk, 256), f32), jax.ShapeDtypeStruct((b, nblk, 256), f32)),
        grid=(b,),
        in_specs=xs + [cst(pe2.shape), cst(w1c.shape), cst(w2c.shape), cst(gkc.shape)],
        out_specs=(pl.BlockSpec((1, nblk, 256), lambda b_: (b_, 0, 0)), pl.BlockSpec((1, nblk, 256), lambda b_: (b_, 0, 0))),
        compiler_params=_cp(("arbitrary",)),
        name="compress_prompt",
    )(kvc, kvc, kvc, kvc, pe2, w1c, w2c, gkc)


def _compress_sample_kernel(pt_ref, *refs, pages_macro):
    pages = refs[:PAGES_PER_STEP]
    pe_ref, w1h_ref, w1l_ref, w2h_ref, w2l_ref, gk_ref, kc_ref, vc_ref, x0, x1, x2, x3 = refs[PAGES_PER_STEP:]
    st = pl.program_id(2)
    xs = [x0, x1, x2, x3]
    psz = pages[0].shape[2]
    for kk in range(PAGES_PER_STEP):
        r0 = pl.multiple_of((st * PAGES_PER_STEP + kk) * psz, psz)
        for c in range(4):
            xs[c][pl.ds(r0, psz), :] = pages[kk][0, 0, :, c * LANES:(c + 1) * LANES]

    @pl.when(st == pl.num_programs(2) - 1)
    def _():
        _compress_core(xs, pe_ref, (w1h_ref, w1l_ref), (w2h_ref, w2l_ref), gk_ref, kc_ref, vc_ref,
                       pages_macro * psz // CMP_BLOCK)


def compress_sample(page_flat, cache4, layer, nb, n_pages, pe2, w1cs, w2cs, gkc):
    psz = cache4.shape[2]
    pages_macro = min(n_pages, 64)
    n_macro = n_pages // pages_macro
    steps = pages_macro // PAGES_PER_STEP
    nblk_macro = pages_macro * psz // CMP_BLOCK
    nblk = n_pages * psz // CMP_BLOCK

    def page_map(b_, m, s, pt, kk):
        return (layer, pt[b_ * n_pages + m * pages_macro + s * PAGES_PER_STEP + kk], 0, 0)

    cst = lambda shape: pl.BlockSpec(shape, lambda b_, m, s, pt: (0,) * len(shape), pipeline_mode=pl.Buffered(1))
    pspecs = [pl.BlockSpec((1, 1, psz, 512), functools.partial(page_map, kk=kk)) for kk in range(PAGES_PER_STEP)]
    ospec = pl.BlockSpec((1, nblk_macro, 256), lambda b_, m, s, pt: (b_, m, 0))
    return pl.pallas_call(
        functools.partial(_compress_sample_kernel, pages_macro=pages_macro),
        out_shape=(jax.ShapeDtypeStruct((nb, nblk, 256), f32), jax.ShapeDtypeStruct((nb, nblk, 256), f32)),
        grid_spec=pltpu.PrefetchScalarGridSpec(
            num_scalar_prefetch=1, grid=(nb, n_macro, steps),
            in_specs=pspecs + [cst(pe2.shape), cst(w1cs[0].shape), cst(w1cs[1].shape),
                               cst(w2cs[0].shape), cst(w2cs[1].shape), cst(gkc.shape)],
            out_specs=(ospec, ospec),
            scratch_shapes=[pltpu.VMEM((pages_macro * psz, LANES), f32) for _ in range(4)]),
        compiler_params=_cp(("arbitrary", "arbitrary", "arbitrary")),
        name="compress_sample",
    )(page_flat, *([cache4] * PAGES_PER_STEP), pe2, *w1cs, *w2cs, gkc)


def _stack_heads(qz, rows):
    return jnp.concatenate([qz[:, h * LANES:(h + 1) * LANES] for h in range(8)], axis=0)


def _unstack_heads(o, rows):
    lane = lax.broadcasted_iota(i32, (rows, LANES), 1)
    left = lane < NSA_DH
    outs = []
    for p in range(4):
        a = o[(2 * p) * rows:(2 * p + 1) * rows]
        b = o[(2 * p + 1) * rows:(2 * p + 2) * rows]
        if p < 2:
            outs.append(jnp.where(left, a, pltpu.roll(b, NSA_DH, 1)))
        else:
            outs.append(jnp.where(left, pltpu.roll(a, NSA_DH, 1), b))
    return jnp.concatenate(outs, axis=1)


def _rank_loop(sc_ref, nblk):
    shape = sc_ref.shape
    blk = lax.broadcasted_iota(i32, shape, 0)
    sc = sc_ref[...]

    def body(j, rank):
        rj = jnp.broadcast_to(sc_ref[pl.ds(j, 1), :], shape)
        beats = (rj > sc) | ((rj == sc) & (blk > j))
        return rank + jnp.where(beats, 1, 0)

    return lax.fori_loop(0, nblk, body, jnp.zeros(shape, i32))


def _cmpsel_prompt_kernel(qz_ref, kc_ref, vc_ref, oc_ref, sel_ref, sc_ref):
    qb = pl.program_id(1)
    tq = qz_ref.shape[1]
    nf = kc_ref.shape[1]
    tpos = qb * tq + lax.broadcasted_iota(i32, (8 * tq, nf), 0) % tq
    blk_end = (lax.broadcasted_iota(i32, (8 * tq, nf), 1) + 1) * CMP_BLOCK - 1
    vis = blk_end <= tpos
    for kp in range(2):
        q = _stack_heads(qz_ref[0, :, kp * 8 * LANES:(kp + 1) * 8 * LANES], tq)
        kc = kc_ref[0, :, kp * LANES:(kp + 1) * LANES].astype(bf16)
        vc = vc_ref[0, :, kp * LANES:(kp + 1) * LANES].astype(bf16)
        s = jnp.where(vis, _dot_nt(q, kc), NEG_INF)
        e = jnp.where(vis, jnp.exp(s - jnp.max(s, axis=1, keepdims=True)), 0.0)
        p = e / jnp.maximum(jnp.sum(e, axis=1, keepdims=True), 1e-30)
        oc_ref[0, :, kp * 4 * LANES:(kp + 1) * 4 * LANES] = _unstack_heads(_dot(p.astype(bf16), vc), tq)
        imp_a = p[0:tq] + p[tq:2 * tq] + p[2 * tq:3 * tq] + p[3 * tq:4 * tq]
        imp_b = p[4 * tq:5 * tq] + p[5 * tq:6 * tq] + p[6 * tq:7 * tq] + p[7 * tq:8 * tq]
        imp_t = jnp.concatenate([imp_a, imp_b], axis=1).T
        blk = lax.broadcasted_iota(i32, (nf, tq), 0)
        cur = (qb * tq + lax.broadcasted_iota(i32, (nf, tq), 1)) // CMP_BLOCK
        forced = (blk == 0) | (blk == cur) | (blk == cur - 1)
        sels = []
        for half in range(2):
            sc = jnp.where(blk <= cur, jnp.where(forced, FORCED_SCORE, imp_t[half * nf:(half + 1) * nf]), NEG_INF)
            sc_ref[...] = sc
            rank = _rank_loop(sc_ref, nf)
            sels.append(jnp.where((rank < TOP_N) & (sc > 0.5 * NEG_INF), 1.0, 0.0))
        sel_ref[0, :, kp * 2 * nf:(kp + 1) * 2 * nf] = jnp.concatenate(sels, axis=0).T.astype(bf16)


def cmpsel_prompt(qz, kc, vc):
    b, t, _ = qz.shape
    nf = kc.shape[1]
    tq = Q_BLOCK
    return pl.pallas_call(
        _cmpsel_prompt_kernel,
        out_shape=(jax.ShapeDtypeStruct((b, t, NSA_HEADS * NSA_DH), f32),
                   jax.ShapeDtypeStruct((b, t, NSA_KV_HEADS * nf), bf16)),
        grid=(b, t // tq),
        in_specs=[pl.BlockSpec((1, tq, 2 * NSA_HEADS * NSA_DH), lambda b_, i: (b_, i, 0)),
                  pl.BlockSpec((1, nf, 256), lambda b_, i: (b_, 0, 0)),
                  pl.BlockSpec((1, nf, 256), lambda b_, i: (b_, 0, 0))],
        out_specs=(pl.BlockSpec((1, tq, NSA_HEADS * NSA_DH), lambda b_, i: (b_, i, 0)),
                   pl.BlockSpec((1, tq, NSA_KV_HEADS * nf), lambda b_, i: (b_, i, 0))),
        scratch_shapes=[pltpu.VMEM((nf, tq), f32)],
        compiler_params=_cp(("arbitrary", "arbitrary")),
        name="cmpsel_prompt",
    )(qz, kc, vc)


def _attn_prompt_kernel(*refs, mode, kt):
    if mode == "slc":
        qz_ref, k_ref, v_ref, sel_ref, o_ref, m_ref, l_ref, acc_ref = refs
    else:
        qz_ref, k_ref, v_ref, o_ref, m_ref, l_ref, acc_ref = refs
    qb = pl.program_id(2)
    tq = qz_ref.shape[1]
    tpos = qb * tq + lax.broadcasted_iota(i32, (tq, kt), 0)
    kiota = lax.broadcasted_iota(i32, (tq, kt), 1)
    m_ref[...] = jnp.full(m_ref.shape, NEG_INF, f32)
    l_ref[...] = jnp.zeros(l_ref.shape, f32)
    acc_ref[...] = jnp.zeros(acc_ref.shape, f32)

    def tile(k0):
        k = k_ref[0, pl.ds(k0, kt), :].astype(bf16)
        v = v_ref[0, pl.ds(k0, kt), :].astype(bf16)
        kpos = k0 + kiota
        if mode == "slc":
            nf = sel_ref.shape[2] // 2
            er = lax.broadcasted_iota(i32, (2 * nf, kt), 0)
            ec = (k0 + lax.broadcasted_iota(i32, (2 * nf, kt), 1)) // CMP_BLOCK
            sel = sel_ref[0]
            ma = _dot(sel, jnp.where(er == ec, 1.0, 0.0).astype(bf16))
            mb = _dot(sel, jnp.where(er - nf == ec, 1.0, 0.0).astype(bf16))
            causal = kpos <= tpos
            biases = (jnp.where(causal & (ma > 0.5), 0.0, NEG_INF), jnp.where(causal & (mb > 0.5), 0.0, NEG_INF))
        else:
            ba = jnp.where((kpos <= tpos) & (kpos >= tpos - WINDOW), 0.0, NEG_INF)
            biases = (ba, ba)
        for h in range(8):
            s = _dot_nt(qz_ref[0, :, h * LANES:(h + 1) * LANES], k) + biases[h // 4]
            m_old = m_ref[h]
            m_new = jnp.maximum(m_old, jnp.max(s, axis=1, keepdims=True))
            p = jnp.exp(s - m_new)
            alpha = jnp.exp(m_old - m_new)
            l_ref[h] = alpha * l_ref[h] + jnp.sum(p, axis=1, keepdims=True)
            acc_ref[h] = alpha * acc_ref[h] + _dot(p.astype(bf16), v)
            m_ref[h] = m_new

    if mode == "slc":
        def body(t_i, c):
            tile(pl.multiple_of(t_i * kt, kt))
            return c
        lax.fori_loop(0, (qb * tq + tq + kt - 1) // kt, body, 0)
    else:
        tile(pl.multiple_of(jnp.maximum(qb * tq - WINDOW, 0), tq))
    o = jnp.concatenate([acc_ref[h] / l_ref[h] for h in range(8)], axis=0)
    o_ref[0] = _unstack_heads(o, tq)


def attn_prompt(qz, kv, sel, mode):
    b, t, _ = qz.shape
    tq = Q_BLOCK
    if mode == "slc":
        kt, kcol, vcol = 256, 4, 6
    else:
        kt, kcol, vcol = WINDOW + tq, 0, 2
        assert t >= kt
    in_specs = [pl.BlockSpec((1, tq, 8 * LANES), lambda b_, kp, i: (b_, i, kp)),
                pl.BlockSpec((1, t, LANES), lambda b_, kp, i: (b_, 0, kcol + kp)),
                pl.BlockSpec((1, t, LANES), lambda b_, kp, i: (b_, 0, vcol + kp))]
    args = [qz, kv, kv]
    if mode == "slc":
        nf2 = sel.shape[2] // 2
        in_specs.append(pl.BlockSpec((1, tq, nf2), lambda b_, kp, i: (b_, i, kp)))
        args.append(sel)
    return pl.pallas_call(
        functools.partial(_attn_prompt_kernel, mode=mode, kt=kt),
        out_shape=jax.ShapeDtypeStruct((b, t, NSA_HEADS * NSA_DH), f32),
        grid=(b, 2, t // tq),
        in_specs=in_specs,
        out_specs=pl.BlockSpec((1, tq, 4 * LANES), lambda b_, kp, i: (b_, i, kp)),
        scratch_shapes=[pltpu.VMEM((8, tq, 1), f32), pltpu.VMEM((8, tq, 1), f32), pltpu.VMEM((8, tq, LANES), f32)],
        compiler_params=_cp(("arbitrary", "arbitrary", "arbitrary")),
        name="attn_" + mode,
    )(*args)


def _cmpsel_sample_kernel(qz_ref, kc_ref, vc_ref, oc_ref, sel_ref, sc_ref, *, q_pos0):
    ts = qz_ref.shape[1]
    nf = kc_ref.shape[1]
    nfp = sc_ref.shape[0]
    rows = 8 * ts
    tpos = q_pos0 + lax.broadcasted_iota(i32, (rows, nf), 0) % ts
    vis = (lax.broadcasted_iota(i32, (rows, nf), 1) + 1) * CMP_BLOCK - 1 <= tpos
    for kp in range(2):
        q = _stack_heads(qz_ref[0, :, kp * 8 * LANES:(kp + 1) * 8 * LANES], ts)
        kc = kc_ref[0, :, kp * LANES:(kp + 1) * LANES]
        vc = vc_ref[0, :, kp * LANES:(kp + 1) * LANES]
        s = jnp.where(vis, _dotp(_dot_nt, q, kc, True), NEG_INF)
        e = jnp.where(vis, jnp.exp(s - jnp.max(s, axis=1, keepdims=True)), 0.0)
        p = e / jnp.maximum(jnp.sum(e, axis=1, keepdims=True), 1e-30)
        oc_ref[0, :, kp * 4 * LANES:(kp + 1) * 4 * LANES] = _unstack_heads(_dotp(_dot, p, vc, True), ts)
        imp_a = p[0:ts] + p[ts:2 * ts] + p[2 * ts:3 * ts] + p[3 * ts:4 * ts]
        imp_b = p[4 * ts:5 * ts] + p[5 * ts:6 * ts] + p[6 * ts:7 * ts] + p[7 * ts:8 * ts]
        imp = jnp.concatenate([imp_a] * 4 + [imp_b] * 4, axis=0)
        if nfp > nf:
            imp = jnp.concatenate([imp, jnp.zeros((rows, nfp - nf), f32)], axis=1)
        imp_t = jnp.concatenate([imp[:, c * LANES:(c + 1) * LANES].T for c in range(nfp // LANES)], axis=0)
        blk = lax.broadcasted_iota(i32, (nfp, rows), 0)
        cur = (q_pos0 + lax.broadcasted_iota(i32, (nfp, rows), 1) % ts) // CMP_BLOCK
        forced = (blk == 0) | (blk == cur) | (blk == cur - 1)
        sc = jnp.where(blk < nf, jnp.where(forced, FORCED_SCORE, imp_t), NEG_INF)
        sc_ref[...] = sc
        rank = _rank_loop(sc_ref, nf)
        sel_ref[0, kp] = jnp.where((rank < TOP_N - 1) & (sc > 0.5 * NEG_INF), 1.0, 0.0)


def cmpsel_sample(qz3, kc, vc, q_pos0):
    nb, ts, _ = qz3.shape
    nf = kc.shape[1]
    nfp = -(-nf // LANES) * LANES
    return pl.pallas_call(
        functools.partial(_cmpsel_sample_kernel, q_pos0=q_pos0),
        out_shape=(jax.ShapeDtypeStruct((nb, ts, NSA_HEADS * NSA_DH), f32),
                   jax.ShapeDtypeStruct((nb, 2, nfp, 8 * ts), f32)),
        grid=(nb,),
        in_specs=[pl.BlockSpec((1, ts, 2 * NSA_HEADS * NSA_DH), lambda b_: (b_, 0, 0)),
                  pl.BlockSpec((1, nf, 256), lambda b_: (b_, 0, 0)),
                  pl.BlockSpec((1, nf, 256), lambda b_: (b_, 0, 0))],
        out_specs=(pl.BlockSpec((1, ts, NSA_HEADS * NSA_DH), lambda b_: (b_, 0, 0)),
                   pl.BlockSpec((1, 2, nfp, 8 * ts), lambda b_: (b_, 0, 0, 0))),
        scratch_shapes=[pltpu.VMEM((nfp, 8 * ts), f32)],
        compiler_params=_cp(("arbitrary",)),
        name="cmpsel_sample",
    )(qz3, kc, vc)


def _slc_sample_kernel(pt_ref, *refs, n_new):
    npg = PAGES_PER_STEP
    kpages, vpages = refs[:npg], refs[npg:2 * npg]
    qz_ref, sel_ref, kn_ref, vn_ref, o_ref, m_ref, l_ref, acc_ref = refs[2 * npg:]
    st = pl.program_id(2)
    ts = qz_ref.shape[1]
    cols = 8 * ts
    q = _stack_heads(qz_ref[0], ts)

    @pl.when(st == 0)
    def _():
        m_ref[...] = jnp.full(m_ref.shape, NEG_INF, f32)
        l_ref[...] = jnp.zeros(l_ref.shape, f32)
        acc_ref[...] = jnp.zeros(acc_ref.shape, f32)

    def update(k, v, bias):
        s = _dotp(_dot_nt, k, q, True) + bias
        m_old = m_ref[...]
        m_new = jnp.maximum(m_old, jnp.max(s, axis=0, keepdims=True))
        p = jnp.exp(s - m_new)
        alpha = jnp.exp(m_old - m_new)
        l_ref[...] = alpha * l_ref[...] + jnp.sum(p, axis=0, keepdims=True)
        acc_ref[...] = alpha * acc_ref[...] + _dotp(_dot_tn, v, p, True)
        m_ref[...] = m_new

    k = jnp.concatenate([r[0, 0] for r in kpages], axis=0)
    v = jnp.concatenate([r[0, 0] for r in vpages], axis=0)
    psz = kpages[0].shape[2]
    bps = npg * psz // CMP_BLOCK
    sel = sel_ref[0, 0]
    er = lax.broadcasted_iota(i32, (bps * CMP_BLOCK, bps), 0) // CMP_BLOCK
    ec = lax.broadcasted_iota(i32, (bps * CMP_BLOCK, bps), 1)
    msk = _dot(jnp.where(er == ec, 1.0, 0.0).astype(bf16), sel.astype(bf16))
    bias = jnp.where(msk > 0.5, 0.0, NEG_INF)
    update(k, v, bias)

    @pl.when(st == pl.num_programs(2) - 1)
    def _():
        kn = kn_ref[0]
        vn = vn_ref[0]
        j = lax.broadcasted_iota(i32, (ts, cols), 0)
        tok = lax.broadcasted_iota(i32, (ts, cols), 1) % ts
        update(kn, vn, jnp.where((j <= tok) & (j < n_new), 0.0, NEG_INF))
        o = (acc_ref[...] / l_ref[...]).T
        o_ref[0] = _unstack_heads(o, ts)


def slc_sample(page_flat, cache4, layer, n_pages, qz3, sel_t, kvc3, n_new):
    nb, ts, _ = qz3.shape
    psz = cache4.shape[2]
    steps = n_pages // PAGES_PER_STEP
    bps = PAGES_PER_STEP * psz // CMP_BLOCK
    cols = 8 * ts

    def page_map(b_, kp, s, pt, kk, col):
        return (layer, pt[b_ * n_pages + s * PAGES_PER_STEP + kk], 0, col + kp)

    kspecs = [pl.BlockSpec((1, 1, psz, LANES), functools.partial(page_map, kk=kk, col=4)) for kk in range(PAGES_PER_STEP)]
    vspecs = [pl.BlockSpec((1, 1, psz, LANES), functools.partial(page_map, kk=kk, col=6)) for kk in range(PAGES_PER_STEP)]
    return pl.pallas_call(
        functools.partial(_slc_sample_kernel, n_new=n_new),
        out_shape=jax.ShapeDtypeStruct((nb, ts, NSA_HEADS * NSA_DH), f32),
        grid_spec=pltpu.PrefetchScalarGridSpec(
            num_scalar_prefetch=1, grid=(nb, 2, steps),
            in_specs=kspecs + vspecs + [
                pl.BlockSpec((1, ts, 8 * LANES), lambda b_, kp, s, pt: (b_, 0, kp)),
                pl.BlockSpec((1, 1, bps, cols), lambda b_, kp, s, pt: (b_, kp, s, 0)),
                pl.BlockSpec((1, ts, LANES), lambda b_, kp, s, pt: (b_, 0, 4 + kp)),
                pl.BlockSpec((1, ts, LANES), lambda b_, kp, s, pt: (b_, 0, 6 + kp))],
            out_specs=pl.BlockSpec((1, ts, 4 * LANES), lambda b_, kp, s, pt: (b_, 0, kp)),
            scratch_shapes=[pltpu.VMEM((1, cols), f32), pltpu.VMEM((1, cols), f32), pltpu.VMEM((LANES, cols), f32)]),
        compiler_params=_cp(("arbitrary", "arbitrary", "arbitrary")),
        name="slc_sample",
    )(page_flat, *([cache4] * (2 * PAGES_PER_STEP)), qz3, sel_t, kvc3, kvc3)


def _win_sample_kernel(qz_ref, old_ref, new_ref, o_ref, nw_ref, *, n_new):
    ts = qz_ref.shape[1]
    wb = old_ref.shape[2]
    rows = 8 * ts
    old = old_ref[0, 0]
    new = new_ref[0]
    tok_o = lax.broadcasted_iota(i32, (rows, wb), 0) % ts
    j_o = lax.broadcasted_iota(i32, (rows, wb), 1)
    bias_o = jnp.where((j_o >= tok_o + (wb - WINDOW)), 0.0, NEG_INF)
    tok_n = lax.broadcasted_iota(i32, (rows, ts), 0) % ts
    j_n = lax.broadcasted_iota(i32, (rows, ts), 1)
    bias_n = jnp.where((j_n <= tok_n) & (j_n < n_new), 0.0, NEG_INF)
    for kp in range(2):
        q = _stack_heads(qz_ref[0, :, kp * 8 * LANES:(kp + 1) * 8 * LANES], ts)
        ko = old[:, kp * LANES:(kp + 1) * LANES]
        vo = old[:, 256 + kp * LANES:256 + (kp + 1) * LANES]
        kn = new[:, kp * LANES:(kp + 1) * LANES]
        vn = new[:, 256 + kp * LANES:256 + (kp + 1) * LANES]
        so = _dotp(_dot_nt, q, ko, True) + bias_o
        sn = _dotp(_dot_nt, q, kn, True) + bias_n
        m = jnp.maximum(jnp.max(so, axis=1, keepdims=True), jnp.max(sn, axis=1, keepdims=True))
        po, pn = jnp.exp(so - m), jnp.exp(sn - m)
        den = jnp.sum(po, axis=1, keepdims=True) + jnp.sum(pn, axis=1, keepdims=True)
        o = (_dotp(_dot, po, vo, True) + _dotp(_dot, pn, vn, True)) / den
        o_ref[0, :, kp * 4 * LANES:(kp + 1) * 4 * LANES] = _unstack_heads(o, ts)
    both = jnp.concatenate([old, new], axis=0)
    nw_ref[0, 0] = pltpu.roll(both, wb + ts - n_new, 0)[0:wb]


def win_sample(qz3, state_win4, layer, kvw3, n_new):
    nb, ts, _ = qz3.shape
    wb = state_win4.shape[2]
    return pl.pallas_call(
        functools.partial(_win_sample_kernel, n_new=n_new),
        out_shape=(jax.ShapeDtypeStruct((nb, ts, NSA_HEADS * NSA_DH), f32),
                   jax.ShapeDtypeStruct((1, nb, wb, 512), f32)),
        grid=(nb,),
        in_specs=[pl.BlockSpec((1, ts, 2 * NSA_HEADS * NSA_DH), lambda b_: (b_, 0, 0)),
                  pl.BlockSpec((1, 1, wb, 512), lambda b_: (layer, b_, 0, 0)),
                  pl.BlockSpec((1, ts, 512), lambda b_: (b_, 0, 0))],
        out_specs=(pl.BlockSpec((1, ts, NSA_HEADS * NSA_DH), lambda b_: (b_, 0, 0)),
                   pl.BlockSpec((1, 1, wb, 512), lambda b_: (0, b_, 0, 0))),
        compiler_params=_cp(("arbitrary",)),
        name="win_sample",
    )(qz3, state_win4, kvw3)


def _mix1_kernel(*refs, nw):
    og_ref, r_ref, ng_ref, mg0_ref, mg1_ref, oc_ref, os_ref, ow_ref, gg_ref = refs[:9]
    wa_refs, wb_refs = refs[9:9 + nw], refs[9 + nw:9 + 2 * nw]
    o_ref = refs[9 + 2 * nw]
    a_refs, b_refs = refs[10 + 2 * nw:10 + 3 * nw], refs[10 + 3 * nw:]
    j = pl.program_id(2)

    @pl.when(j == 0)
    def _():
        ys = []
        for h in range(GLA_HEADS):
            hs = slice(h * GLA_DV, (h + 1) * GLA_DV)
            x = og_ref[0, :, hs]
            y = x * lax.rsqrt(jnp.mean(x * x, axis=-1, keepdims=True) + EPS) * gg_ref[:, hs]
            ys.append(y * _silu(r_ref[0, :, hs]))
        _store_split(jnp.concatenate(ys, axis=1), a_refs)
        ghi, glo = _split2(jax.nn.sigmoid(ng_ref[0]))
        nh = NSA_HEADS * NSA_DH
        er = lax.broadcasted_iota(i32, (LANES, nh), 0)
        ec = lax.broadcasted_iota(i32, (LANES, nh), 1) // NSA_DH
        acc = None
        for br, ref in enumerate((oc_ref, os_ref, ow_ref)):
            ex = jnp.where(er == br * NSA_HEADS + ec, 1.0, 0.0).astype(bf16)
            term = (_dot(ghi, ex) + _dot(glo, ex)) * ref[0]
            acc = term if acc is None else acc + term
        _store_split(acc, b_refs)

    o_ref[0] = (mg0_ref[0] * _mm_split(a_refs, wa_refs) + mg1_ref[0] * _mm_split(b_refs, wb_refs)).astype(o_ref.dtype)


def mix1(o_gla, z, o_cmp, o_slc, o_win, gg, was, wbs):
    grp, t, n1 = o_gla.shape
    d = was[0].shape[1]
    nw = len(was)
    tm, tn = min(t, 512), 1024
    row = lambda w, c: pl.BlockSpec((1, tm, w), functools.partial(lambda g_, i, j, c: (g_, i, c), c=c))
    return pl.pallas_call(
        functools.partial(_mix1_kernel, nw=nw),
        out_shape=jax.ShapeDtypeStruct((grp, t, d), bf16 if nw == 1 else f32),
        grid=(grp, t // tm, d // tn),
        in_specs=[row(n1, 0), row(1024, C_GR // 1024), row(LANES, C_NG // LANES),
                  pl.BlockSpec((1, tm, tn), lambda g_, i, j: (g_, i, C_MG // tn + j)),
                  pl.BlockSpec((1, tm, tn), lambda g_, i, j: (g_, i, (C_MG + d) // tn + j)),
                  row(n1, 0), row(n1, 0), row(n1, 0),
                  pl.BlockSpec((1, n1), lambda g_, i, j: (0, 0))]
        + [pl.BlockSpec((n1, tn), lambda g_, i, j: (0, j))] * (2 * nw),
        out_specs=pl.BlockSpec((1, tm, tn), lambda g_, i, j: (g_, i, j)),
        scratch_shapes=[pltpu.VMEM((tm, n1), bf16)] * (2 * nw),
        compiler_params=_cp(("arbitrary", "arbitrary", "arbitrary")),
        name="mix1",
    )(o_gla, z, z, z, z, o_cmp, o_slc, o_win, gg, *was, *wbs)


def _mm_res_kernel(*refs, nw):
    a_ref = refs[0]
    w_refs = refs[1:1 + nw]
    x_ref, gt_ref, o_ref = refs[1 + nw:]
    o_ref[0] = x_ref[0] + gt_ref[0] * _mm(a_ref[0], [w[...] for w in w_refs])


def mm_res(a, ws, x, gate):
    grp, t, k = a.shape
    d = ws[0].shape[1]
    nw = len(ws)
    tm, tn = min(t, 512), 512
    per_row = gate.shape[1] != 1
    return pl.pallas_call(
        functools.partial(_mm_res_kernel, nw=nw),
        out_shape=jax.ShapeDtypeStruct((grp, t, d), f32),
        grid=(grp, t // tm, d // tn),
        in_specs=[pl.BlockSpec((1, tm, k), lambda g_, i, j: (g_, i, 0))]
        + [pl.BlockSpec((k, tn), lambda g_, i, j: (0, j))] * nw
        + [pl.BlockSpec((1, tm, tn), lambda g_, i, j: (g_, i, j)),
           pl.BlockSpec((1, tm, tn), lambda g_, i, j: (g_, i, j)) if per_row
           else pl.BlockSpec((1, 1, tn), lambda g_, i, j: (g_, 0, j))],
        out_specs=pl.BlockSpec((1, tm, tn), lambda g_, i, j: (g_, i, j)),
        compiler_params=_cp(("arbitrary", "arbitrary", "arbitrary")),
        name="mm_res",
    )(a, *ws, x, gate)


def _ffn1_kernel(*refs, nw):
    x_ref, g_ref, sh_ref, sc_ref = refs[:4]
    wg_refs, wu_refs = refs[4:4 + nw], refs[4 + nw:4 + 2 * nw]
    o_ref = refs[4 + 2 * nw]
    h_refs = refs[5 + 2 * nw:]

    @pl.when(pl.program_id(2) == 0)
    def _():
        _store_split(_modulated(x_ref, g_ref, sh_ref, sc_ref), h_refs)

    o_ref[0] = (_silu(_mm_split(h_refs, wg_refs)) * _mm_split(h_refs, wu_refs)).astype(o_ref.dtype)


def ffn1(x, g, shift, scale, wgs, wus):
    grp, t, d = x.shape
    f = wgs[0].shape[1]
    nw = len(wgs)
    tm, tn = min(t, 512), 512
    per_row = shift.shape[1] != 1
    return pl.pallas_call(
        functools.partial(_ffn1_kernel, nw=nw),
        out_shape=jax.ShapeDtypeStruct((grp, t, f), bf16 if nw == 1 else f32),
        grid=(grp, t // tm, f // tn),
        in_specs=[pl.BlockSpec((1, tm, d), lambda g_, i, j: (g_, i, 0)),
                  pl.BlockSpec((1, d), lambda g_, i, j: (0, 0)),
                  _mod_specs(t, tm, d, per_row), _mod_specs(t, tm, d, per_row)]
        + [pl.BlockSpec((d, tn), lambda g_, i, j: (0, j))] * (2 * nw),
        out_specs=pl.BlockSpec((1, tm, tn), lambda g_, i, j: (g_, i, j)),
        scratch_shapes=[pltpu.VMEM((tm, d), bf16)] * nw,
        compiler_params=_cp(("arbitrary", "arbitrary", "arbitrary")),
        name="ffn1",
    )(x, g, shift, scale, *wgs, *wus)


def _router_kernel(x_ref, g_ref, sh_ref, sc_ref, whi_ref, wlo_ref, h_ref, r_ref):
    h = _modulated(x_ref, g_ref, sh_ref, sc_ref)
    h_ref[0] = h
    hi, lo = _split2(h)
    logits = _dot(hi, whi_ref[...]) + _dot(lo, whi_ref[...]) + _dot(hi, wlo_ref[...])
    lane = lax.broadcasted_iota(i32, logits.shape, 1)
    lg = jnp.where(lane < N_EXPERTS, logits, -jnp.inf)
    v1 = jnp.max(lg, axis=1, keepdims=True)
    i1 = jnp.min(jnp.where(lg == v1, lane, LANES), axis=1, keepdims=True)
    lg2 = jnp.where(lane == i1, -jnp.inf, lg)
    v2 = jnp.max(lg2, axis=1, keepdims=True)
    i2 = jnp.min(jnp.where(lg2 == v2, lane, LANES), axis=1, keepdims=True)
    e2 = jnp.exp(v2 - v1)
    g1 = 1.0 / (1.0 + e2)
    g2 = e2 / (1.0 + e2)
    r_ref[0] = jnp.where(lane == 0, i1.astype(f32), jnp.where(lane == 1, i2.astype(f32),
                         jnp.where(lane == 2, g1, jnp.where(lane == 3, g2, 0.0))))


def router(x, g, shift, scale, whi, wlo):
    grp, t, d = x.shape
    tm = min(t, 512)
    per_row = shift.shape[1] != 1
    mod = (pl.BlockSpec((1, tm, d), lambda g_, i: (g_, i, 0)) if per_row
           else pl.BlockSpec((1, 1, d), lambda g_, i: (g_, 0, 0)))
    return pl.pallas_call(
        _router_kernel,
        out_shape=(jax.ShapeDtypeStruct((grp, t, d), f32), jax.ShapeDtypeStruct((grp, t, LANES), f32)),
        grid=(grp, t // tm),
        in_specs=[pl.BlockSpec((1, tm, d), lambda g_, i: (g_, i, 0)),
                  pl.BlockSpec((1, d), lambda g_, i: (0, 0)), mod, mod,
                  pl.BlockSpec((d, LANES), lambda g_, i: (0, 0)),
                  pl.BlockSpec((d, LANES), lambda g_, i: (0, 0))],
        out_specs=(pl.BlockSpec((1, tm, d), lambda g_, i: (g_, i, 0)),
                   pl.BlockSpec((1, tm, LANES), lambda g_, i: (g_, i, 0))),
        compiler_params=_cp(("arbitrary", "arbitrary")),
        name="router",
    )(x, g, shift, scale, whi, wlo)


def _gather_rows_kernel(idx_ref, src_ref, dst_ref, sem, *, rows):
    base = pl.program_id(0) * rows

    def row_copy(src_row, dst_row):
        return pltpu.make_async_copy(src_ref.at[pl.ds(src_row, 1)], dst_ref.at[pl.ds(dst_row, 1)], sem)

    def start(r, c):
        row_copy(idx_ref[base + r], r).start()
        return c

    def wait(r, c):
        row_copy(0, r).wait()
        return c

    lax.fori_loop(0, rows, start, 0, unroll=8)
    lax.fori_loop(0, rows, wait, 0, unroll=8)


def gather_rows(idx, src, n_out):
    rows = 512 if n_out % 512 == 0 else 128
    return pl.pallas_call(
        functools.partial(_gather_rows_kernel, rows=rows),
        out_shape=jax.ShapeDtypeStruct((n_out, src.shape[1]), src.dtype),
        grid_spec=pltpu.PrefetchScalarGridSpec(
            num_scalar_prefetch=1, grid=(n_out // rows,),
            in_specs=[pl.BlockSpec(memory_space=pl.ANY)],
            out_specs=pl.BlockSpec((rows, src.shape[1]), lambda i, idx_: (i, 0)),
            scratch_shapes=[pltpu.SemaphoreType.DMA]),
        compiler_params=_cp(("arbitrary",)),
        name="gather_rows",
    )(idx, src)


def _moe1_kernel(te_ref, nt_ref, xs_ref, *refs, nw):
    wg_refs, wu_refs, o_ref = refs[:nw], refs[nw:2 * nw], refs[2 * nw]

    @pl.when(pl.program_id(1) < nt_ref[0])
    def _():
        x = xs_ref[...]
        gate = _mm(x, [w[0] for w in wg_refs])
        o_ref[...] = (_silu(gate) * _mm(x, [w[0] for w in wu_refs])).astype(o_ref.dtype)

    @pl.when(pl.program_id(1) >= nt_ref[0])
    def _():
        o_ref[...] = jnp.zeros(o_ref.shape, o_ref.dtype)


def moe1(tile_expert, n_tiles_used, xs, wgs, wus, tm):
    p, d = xs.shape
    fe = wgs[0].shape[2]
    nw = len(wgs)
    tn = fe // 2 if nw == 1 else 256
    return pl.pallas_call(
        functools.partial(_moe1_kernel, nw=nw),
        out_shape=jax.ShapeDtypeStruct((p, fe), bf16 if nw == 1 else f32),
        grid_spec=pltpu.PrefetchScalarGridSpec(
            num_scalar_prefetch=2, grid=(fe // tn, p // tm),
            in_specs=[pl.BlockSpec((tm, d), lambda j, i, te, nt: (i, 0))]
            + [pl.BlockSpec((1, d, tn), lambda j, i, te, nt: (te[i], 0, j))] * (2 * nw),
            out_specs=pl.BlockSpec((tm, tn), lambda j, i, te, nt: (i, j))),
        compiler_params=_cp(("arbitrary", "arbitrary")),
        name="moe1",
    )(tile_expert, n_tiles_used, xs, *wgs, *wus)


def _moe2_kernel(te_ref, nt_ref, a_ref, *refs, nw):
    wd_refs, sg_ref, o_ref = refs[:nw], refs[nw], refs[nw + 1]

    @pl.when(pl.program_id(1) < nt_ref[0])
    def _():
        o_ref[...] = sg_ref[...] * _mm(a_ref[...], [w[0] for w in wd_refs])

    @pl.when(pl.program_id(1) >= nt_ref[0])
    def _():
        o_ref[...] = jnp.zeros(o_ref.shape, o_ref.dtype)


def moe2(tile_expert, n_tiles_used, act, wds, slot_gate, tm):
    p, fe = act.shape
    d = wds[0].shape[2]
    nw = len(wds)
    tn = d // 2 if nw == 1 else 512
    return pl.pallas_call(
        functools.partial(_moe2_kernel, nw=nw),
        out_shape=jax.ShapeDtypeStruct((p, d), f32),
        grid_spec=pltpu.PrefetchScalarGridSpec(
            num_scalar_prefetch=2, grid=(d // tn, p // tm),
            in_specs=[pl.BlockSpec((tm, fe), lambda j, i, te, nt: (i, 0))]
            + [pl.BlockSpec((1, fe, tn), lambda j, i, te, nt: (te[i], 0, j))] * nw
            + [pl.BlockSpec((tm, 1), lambda j, i, te, nt: (i, 0))],
            out_specs=pl.BlockSpec((tm, tn), lambda j, i, te, nt: (i, j))),
        compiler_params=_cp(("arbitrary", "arbitrary")),
        name="moe2",
    )(tile_expert, n_tiles_used, act, *wds, slot_gate)


def _combine_kernel(x_ref, gt_ref, y1_ref, y2_ref, o_ref):
    o_ref[0] = x_ref[0] + gt_ref[0] * (y1_ref[0] + y2_ref[0])


def combine(x, gate, y1, y2):
    grp, t, d = x.shape
    tm = min(t, 512)
    per_row = gate.shape[1] != 1
    row = pl.BlockSpec((1, tm, d), lambda g_, i: (g_, i, 0))
    return pl.pallas_call(
        _combine_kernel,
        out_shape=jax.ShapeDtypeStruct((grp, t, d), f32),
        grid=(grp, t // tm),
        in_specs=[row, row if per_row else pl.BlockSpec((1, 1, d), lambda g_, i: (g_, 0, 0)), row, row],
        out_specs=row,
        compiler_params=_cp(("arbitrary", "arbitrary")),
        name="combine",
    )(x, gate, y1, y2)


def moe_layer(x, g, shift, scale, gate, whi, wlo, wgs, wus, wds):
    grp, t, d = x.shape
    n = grp * t
    tm = 512 if n >= 4096 else 128
    h, route = router(x, g, shift, scale, whi, wlo)
    route = route.reshape(n, LANES)
    e_idx = route[:, 0:TOP_K].astype(i32)
    gates = route[:, 2:2 + TOP_K]
    e_flat = e_idx.reshape(-1)
    onehot = (e_flat[:, None] == jnp.arange(N_EXPERTS, dtype=i32)[None, :]).astype(i32)
    within = jnp.cumsum(onehot, axis=0) - onehot
    counts = jnp.sum(onehot, axis=0)
    padded = (counts + tm - 1) // tm * tm
    starts = jnp.cumsum(padded) - padded
    slot = jnp.sum(onehot * (starts[None, :] + within), axis=1)
    p = TOP_K * n + N_EXPERTS * tm
    p = -(-p // 256) * 256
    slot_tok = jnp.zeros((p,), i32).at[slot].set(jnp.arange(TOP_K * n, dtype=i32) // TOP_K)
    slot_gate = jnp.zeros((p,), f32).at[slot].set(gates.reshape(-1))
    n_tiles = p // tm
    ends = jnp.cumsum(padded)
    tile_start = jnp.arange(n_tiles, dtype=i32) * tm
    tile_expert = jnp.minimum(jnp.sum((tile_start[:, None] >= ends[None, :]).astype(i32), axis=1), N_EXPERTS - 1).astype(i32)
    n_used = (ends[-1] // tm).astype(i32).reshape(1)
    xs = gather_rows(slot_tok, h.reshape(n, d), p)
    act = moe1(tile_expert, n_used, xs, wgs, wus, tm)
    ys = moe2(tile_expert, n_used, act, wds, slot_gate.reshape(p, 1), tm)
    slots = slot.reshape(n, TOP_K)
    y1 = gather_rows(slots[:, 0], ys, n).reshape(grp, t, d)
    y2 = gather_rows(slots[:, 1], ys, n).reshape(grp, t, d)
    return combine(x, gate, y1, y2)


def _rope_tables(pos):
    half = ROPE_DIMS // 2
    inv = ROPE_THETA ** (-(jnp.arange(half, dtype=f32) * 2.0 / ROPE_DIMS))
    ang = pos.astype(f32)[:, None] * inv[None, :]
    cos, sin = jnp.cos(ang), jnp.sin(ang)
    n = pos.shape[0]
    pad = jnp.zeros((n, NSA_DH - ROPE_DIMS), f32)
    zero = jnp.zeros((n, half), f32)
    c64 = jnp.concatenate([cos, cos, pad + 1.0], axis=1)
    a64 = jnp.concatenate([-sin, zero, pad], axis=1)
    b64 = jnp.concatenate([zero, sin, pad], axis=1)
    return tuple(jnp.tile(x, (1, 2)) for x in (c64, a64, b64))


def _hilo(w):
    hi = lax.reduce_precision(w, exponent_bits=8, mantissa_bits=7)
    return hi.astype(bf16), (w - hi).astype(bf16)


def _pair_diag(w):
    z = jnp.zeros_like(w)
    return jnp.concatenate([jnp.concatenate([w, z], axis=-1), jnp.concatenate([z, w], axis=-1)], axis=-2)


def _mods(mod_l, rows, per_row_t):
    d = mod_l.shape[1] // 6
    m = mod_l.reshape(rows, 2, 3, d)
    out = {}
    for si, sname in enumerate(("mix", "ffn")):
        for ti, tname in enumerate(("shift", "scale", "gate")):
            v = m[:, si, ti]
            if per_row_t:
                v = jnp.repeat(v, per_row_t, axis=0)[None]
            else:
                v = v[:, None, :]
            out[sname + "_" + tname] = v
    return out


def kernel(x_prompt, x_sample, cache_kv, state_kv_win, state_gla, page_table, c_prompt, c_sample, norm_g, w_ada, b_ada, w_in, w_alpha2, b_alpha, gla_norm_g, qk_g, phi_pe, phi_w1, phi_w2, w_branch_a, w_branch_b, w_out, w_ffn_gate, w_ffn_up, w_ffn_down, w_router, w_exp_gate, w_exp_up, w_exp_down):
    bp, tp, d = x_prompt.shape
    bs, t_new, _ = x_sample.shape
    depth = w_in.shape[0]
    n_pool, psz = cache_kv.shape[1], cache_kv.shape[2]
    n_pages = page_table.shape[1]
    past_len = n_pages * psz
    wbuf = state_kv_win.shape[2]
    ts = TS_PAD

    w_in_r = _hilo(jnp.concatenate([
        w_in[:, :, 0:2048], w_in[:, :, 2064:3088], w_in[:, :, 3088:4112], w_in[:, :, 4112:5648],
        w_in[:, :, 5648:5696], w_in[:, :, 2048:2064], jnp.zeros((depth, d, C_MG - C_GA - GLA_RANK), f32),
        w_in[:, :, 5696:9792]], axis=2))
    ba2 = b_alpha.reshape(depth, 1, -1)
    wba, wbb, wo = _hilo(w_branch_a), _hilo(w_branch_b), _hilo(w_out)
    wfg, wfu, wfd = _hilo(w_ffn_gate), _hilo(w_ffn_up), _hilo(w_ffn_down)
    weg, weu, wed = _hilo(w_exp_gate), _hilo(w_exp_up), _hilo(w_exp_down)
    wr_hi, wr_lo = _hilo(jnp.pad(w_router, ((0, 0), (0, 0), (0, LANES - N_EXPERTS))))
    w1c = _hilo(_pair_diag(phi_w1.reshape(depth, 2, CMP_BLOCK, NSA_DH, -1)))
    w2c = _hilo(_pair_diag(phi_w2))
    pe2 = jnp.tile(phi_pe, (1, 1, 1, 2))
    gq = jnp.tile(qk_g[:, 0:1, :], (1, 1, 2))
    gk = jnp.tile(qk_g[:, 1:4, :], (1, 1, 2))
    gkc = jnp.tile(qk_g[:, 4:5, :], (1, 1, 2))
    ggl = gla_norm_g.reshape(depth, 1, -1)
    ng = norm_g.reshape(depth, 2, 1, d)

    c_pad = jnp.zeros((16, d), f32).at[:bp].set(c_prompt).at[bp:bp + bs].set(c_sample)
    mod_all = ada_all(c_pad, w_ada, b_ada)

    cos_p, sa_p, sb_p = _rope_tables(jnp.arange(tp, dtype=i32))
    pos_s = past_len + (jnp.arange(bs * ts, dtype=i32) % ts)
    cos_s, sa_s, sb_s = _rope_tables(pos_s)

    page_flat = page_table.reshape(-1).astype(i32)
    cache4 = cache_kv.reshape(depth, n_pool, psz, 4 * NSA_KV_HEADS * NSA_DH)
    win4 = state_kv_win.reshape(depth, bs, wbuf, 2 * NSA_KV_HEADS * NSA_DH)

    x_p = x_prompt
    x_s = jnp.zeros((bs, ts, d), f32).at[:, :t_new].set(x_sample).reshape(1, bs * ts, d)
    kv_p, win_p, gla_p, kv_s, win_s, gla_s = [], [], [], [], [], []
    hi = lambda w, i: (w[0][i],)
    hl = lambda w, i: (w[0][i], w[1][i])
    flat = lambda a: a.reshape(1, bs * ts, -1)
    for l in range(depth):
        mp = _mods(mod_all[l, :bp], bp, 0)
        ms = _mods(mod_all[l, bp:bp + bs], bs, ts)
        z = proj(x_p, ng[l, 0], mp["mix_shift"], mp["mix_scale"], hi(w_in_r, l))
        qz, kvc, kvw = nsa_prep(z, cos_p, sa_p, sb_p, gq[l], gk[l], bf16)
        o_gla, st = gla(z, w_alpha2[l], ba2[l], None, GLA_CHUNK, GLA_SUB, GLA_CHUNK, False)
        kc, vc = compress_prompt(kvc, pe2[l], w1c[0][l], w2c[0][l], gkc[l])
        o_cmp, sel = cmpsel_prompt(qz, kc, vc)
        o_slc = attn_prompt(qz, kvc, sel, "slc")
        o_win = attn_prompt(qz, kvw, None, "win")
        merged = mix1(o_gla, z, o_cmp, o_slc, o_win, ggl[l], hi(wba, l), hi(wbb, l))
        x_p = mm_res(merged, hi(wo, l), x_p, mp["mix_gate"])
        kv_p.append(kvc.reshape(bp, tp, 4, NSA_KV_HEADS, NSA_DH))
        win_p.append(kvw[:, tp - WINDOW:].reshape(bp, WINDOW, 2, NSA_KV_HEADS, NSA_DH))
        gla_p.append(st)
        z = proj(x_s, ng[l, 0], ms["mix_shift"], ms["mix_scale"], hl(w_in_r, l))
        qz, kvc, kvw = nsa_prep(z, cos_s, sa_s, sb_s, gq[l], gk[l], f32)
        z3 = z.reshape(bs, ts, N_PROJ)
        o_gla, st = gla(z3, w_alpha2[l], ba2[l], state_gla[l], ts, ts, t_new, True)
        qz3 = qz.reshape(bs, ts, -1)
        kvc3 = kvc.reshape(bs, ts, -1)
        kvw3 = kvw.reshape(bs, ts, -1)
        kc, vc = compress_sample(page_flat, cache4, l, bs, n_pages, pe2[l], hl(w1c, l), hl(w2c, l), gkc[l])
        o_cmp, sel_t = cmpsel_sample(qz3, kc, vc, past_len)
        o_slc = slc_sample(page_flat, cache4, l, n_pages, qz3, sel_t, kvc3, t_new)
        o_win, new_win = win_sample(qz3, win4, l, kvw3, t_new)
        merged = mix1(flat(o_gla), z, flat(o_cmp), flat(o_slc), flat(o_win), ggl[l], hl(wba, l), hl(wbb, l))
        x_s = mm_res(merged, hl(wo, l), x_s, ms["mix_gate"])
        kv_s.append(kvc3[:, :t_new].reshape(bs, t_new, 4, NSA_KV_HEADS, NSA_DH))
        win_s.append(new_win.reshape(bs, wbuf, 2, NSA_KV_HEADS, NSA_DH))
        gla_s.append(st)
        e = l // 2
        if l % 2 == 0:
            a = ffn1(x_p, ng[l, 1], mp["ffn_shift"], mp["ffn_scale"], hi(wfg, e), hi(wfu, e))
            x_p = mm_res(a, hi(wfd, e), x_p, mp["ffn_gate"])
            a = ffn1(x_s, ng[l, 1], ms["ffn_shift"], ms["ffn_scale"], hl(wfg, e), hl(wfu, e))
            x_s = mm_res(a, hl(wfd, e), x_s, ms["ffn_gate"])
        else:
            x_p = moe_layer(x_p, ng[l, 1], mp["ffn_shift"], mp["ffn_scale"], mp["ffn_gate"],
                            wr_hi[e], wr_lo[e], hi(weg, e), hi(weu, e), hi(wed, e))
            x_s = moe_layer(x_s, ng[l, 1], ms["ffn_shift"], ms["ffn_scale"], ms["ffn_gate"],
                            wr_hi[e], wr_lo[e], hl(weg, e), hl(weu, e), hl(wed, e))
    y_s = x_s.reshape(bs, ts, d)[:, :t_new]
    return (x_p, y_s, jnp.stack(kv_p), jnp.stack(win_p), jnp.stack(gla_p),
            jnp.stack(kv_s), jnp.stack(win_s), jnp.stack(gla_s))
```

```python
import functools

import jax
import jax.numpy as jnp
from jax import lax
from jax.experimental import pallas as pl
from jax.experimental.pallas import tpu as pltpu

f32 = jnp.float32
bf16 = jnp.bfloat16
i32 = jnp.int32

GLA_HEADS = 4
GLA_DK = 128
GLA_DV = 256
GLA_RANK = 16
GLA_TAU = 16.0
GLA_CHUNK = 64
GLA_SUB = 16
NSA_HEADS = 16
NSA_KV_HEADS = 4
NSA_DH = 64
CMP_BLOCK = 64
TOP_N = 16
WINDOW = 512
Q_BLOCK = 128
ROPE_THETA = 500000.0
ROPE_DIMS = NSA_DH // 4
N_EXPERTS = 8
TOP_K = 2
EPS = 1e-6
NEG_INF = -1e30
FORCED_SCORE = 1e4

LANES = 128
TS_PAD = 16
VMEM_LIMIT = 56 * 1024 * 1024

C_GQ, C_GK, C_GV, C_GR, C_NQ, C_KVC, C_KVW, C_NG, C_GA, C_MG = 0, 512, 1024, 2048, 3072, 4096, 5120, 5632, 5680, 6144
N_PROJ = 10240
PAGES_PER_STEP = 8


def _cp(sem):
    return pltpu.CompilerParams(dimension_semantics=sem, vmem_limit_bytes=VMEM_LIMIT)


def _dot(a, b):
    return jnp.dot(a, b, preferred_element_type=f32)


def _dot_nt(a, b):
    return lax.dot_general(a, b, (((1,), (1,)), ((), ())), preferred_element_type=f32)


def _dot_tn(a, b):
    return lax.dot_general(a, b, (((0,), (0,)), ((), ())), preferred_element_type=f32)


def _split2(x):
    hi = x.astype(bf16)
    lo = (x - hi.astype(f32)).astype(bf16)
    return hi, lo


def _split3(x):
    hi = x.astype(bf16)
    r = x - hi.astype(f32)
    mid = r.astype(bf16)
    lo = (r - mid.astype(f32)).astype(bf16)
    return hi, mid, lo


def _dotp(fn, a, b, hp):
    if not hp:
        return fn(a.astype(bf16), b.astype(bf16))
    ah, al = _split2(a)
    bh, bl = _split2(b)
    return fn(ah, bh) + fn(al, bh) + fn(ah, bl)


def _mm(a, ws):
    if len(ws) == 1:
        return _dot(a.astype(bf16), ws[0])
    ah, al = _split2(a)
    return _dot(ah, ws[0]) + _dot(al, ws[0]) + _dot(ah, ws[1])


def _silu(x):
    return x * jax.nn.sigmoid(x)


def _group_ones(n, group):
    r = lax.broadcasted_iota(i32, (n, n), 0) // group
    c = lax.broadcasted_iota(i32, (n, n), 1) // group
    return jnp.where(r == c, 1.0, 0.0).astype(bf16)


def _rms64(x, gain):
    e = _group_ones(LANES, NSA_DH)
    hi, lo = _split2(x * x)
    ssq = _dot(hi, e) + _dot(lo, e)
    return x * lax.rsqrt(ssq * (1.0 / NSA_DH) + EPS) * gain


def _modulated(x_ref, g_ref, sh_ref, sc_ref):
    x = x_ref[0]
    y = x * lax.rsqrt(jnp.mean(x * x, axis=-1, keepdims=True) + EPS) * g_ref[...]
    return y * (1.0 + sc_ref[0]) + sh_ref[0]


def _mod_specs(t, tm, d, per_row):
    if per_row:
        return pl.BlockSpec((1, tm, d), lambda g, i, j: (g, i, 0))
    return pl.BlockSpec((1, 1, d), lambda g, i, j: (g, 0, 0))


def _ada_kernel(c_ref, w_ref, b_ref, o_ref):
    o_ref[0] = _dotp(_dot, c_ref[...], w_ref[0], True) + b_ref[0]


def ada_all(c_pad, w_ada, b_ada):
    nl, d, n = w_ada.shape
    tn = 1024
    return pl.pallas_call(
        _ada_kernel,
        out_shape=jax.ShapeDtypeStruct((nl, c_pad.shape[0], n), f32),
        grid=(nl, n // tn),
        in_specs=[pl.BlockSpec(c_pad.shape, lambda l, j: (0, 0)),
                  pl.BlockSpec((1, d, tn), lambda l, j: (l, 0, j)),
                  pl.BlockSpec((1, 1, tn), lambda l, j: (l, 0, j))],
        out_specs=pl.BlockSpec((1, c_pad.shape[0], tn), lambda l, j: (l, 0, j)),
        compiler_params=_cp(("arbitrary", "arbitrary")),
        name="ada",
    )(c_pad, w_ada, b_ada.reshape(nl, 1, n))


def _store_split(h, h_refs):
    if len(h_refs) == 1:
        h_refs[0][...] = h.astype(bf16)
    else:
        hi, lo = _split2(h)
        h_refs[0][...] = hi
        h_refs[1][...] = lo


def _mm_split(h_refs, w_refs):
    acc = _dot(h_refs[0][...], w_refs[0][...])
    if len(w_refs) == 2:
        acc = acc + _dot(h_refs[1][...], w_refs[0][...]) + _dot(h_refs[0][...], w_refs[1][...])
    return acc


def _proj_kernel(*refs, sig_from, nw):
    x_ref, g_ref, sh_ref, sc_ref = refs[:4]
    w_refs = refs[4:4 + nw]
    o_ref = refs[4 + nw]
    h_refs = refs[5 + nw:]
    j = pl.program_id(2)

    @pl.when(j == 0)
    def _():
        _store_split(_modulated(x_ref, g_ref, sh_ref, sc_ref), h_refs)

    acc = _mm_split(h_refs, w_refs)

    @pl.when(j < sig_from)
    def _():
        o_ref[0] = acc

    @pl.when(j >= sig_from)
    def _():
        o_ref[0] = jax.nn.sigmoid(acc)


def proj(x, g, shift, scale, ws):
    grp, t, d = x.shape
    n = ws[0].shape[1]
    tm, tn = min(t, 1024), 1024
    per_row = shift.shape[1] != 1
    nw = len(ws)
    return pl.pallas_call(
        functools.partial(_proj_kernel, sig_from=C_MG // tn, nw=nw),
        out_shape=jax.ShapeDtypeStruct((grp, t, n), f32),
        grid=(grp, t // tm, n // tn),
        in_specs=[pl.BlockSpec((1, tm, d), lambda g_, i, j: (g_, i, 0)),
                  pl.BlockSpec((1, d), lambda g_, i, j: (0, 0)),
                  _mod_specs(t, tm, d, per_row), _mod_specs(t, tm, d, per_row)]
        + [pl.BlockSpec((d, tn), lambda g_, i, j: (0, j))] * nw,
        out_specs=pl.BlockSpec((1, tm, tn), lambda g_, i, j: (g_, i, j)),
        scratch_shapes=[pltpu.VMEM((tm, d), bf16)] * nw,
        compiler_params=_cp(("arbitrary", "arbitrary", "arbitrary")),
        name="proj",
    )(x, g, shift, scale, *ws)


def _rope(y, cos, sa, sb):
    return y * cos + pltpu.roll(y, LANES - ROPE_DIMS // 2, 1) * sa + pltpu.roll(y, ROPE_DIMS // 2, 1) * sb


def _nsa_prep_kernel(nq_ref, kvc_ref, kvw_ref, cos_ref, sa_ref, sb_ref, gq_ref, gk_ref, qz_ref, oc_ref, ow_ref):
    cos, sa, sb = cos_ref[...], sa_ref[...], sb_ref[...]
    lane = lax.broadcasted_iota(i32, cos.shape, 1)
    left = lane < NSA_DH
    for m in range(NSA_HEADS // 2):
        y = _rope(_rms64(nq_ref[0, :, m * LANES:(m + 1) * LANES], gq_ref[...]), cos, sa, sb) * (NSA_DH ** -0.5)
        yr = pltpu.roll(y, NSA_DH, 1)
        if (m // 2) % 2 == 0:
            a, b = jnp.where(left, y, 0.0), jnp.where(left, yr, 0.0)
        else:
            a, b = jnp.where(left, 0.0, yr), jnp.where(left, 0.0, y)
        qz_ref[0, :, (2 * m) * LANES:(2 * m + 1) * LANES] = a.astype(qz_ref.dtype)
        qz_ref[0, :, (2 * m + 1) * LANES:(2 * m + 2) * LANES] = b.astype(qz_ref.dtype)
    for c in range(8):
        x = kvc_ref[0, :, c * LANES:(c + 1) * LANES]
        if (c // 2) % 2 == 0:
            x = _rope(_rms64(x, gk_ref[c // 4:c // 4 + 1, :]), cos, sa, sb)
        oc_ref[0, :, c * LANES:(c + 1) * LANES] = x
    for c in range(4):
        x = kvw_ref[0, :, c * LANES:(c + 1) * LANES]
        if c < 2:
            x = _rope(_rms64(x, gk_ref[2:3, :]), cos, sa, sb)
        ow_ref[0, :, c * LANES:(c + 1) * LANES] = x


def nsa_prep(z, cos, sa, sb, gq, gk, q_dtype):
    grp, t, _ = z.shape
    tm = min(t, 256)
    tab = pl.BlockSpec((tm, LANES), lambda g_, i: (i, 0))
    return pl.pallas_call(
        _nsa_prep_kernel,
        out_shape=(jax.ShapeDtypeStruct((grp, t, 2 * NSA_HEADS * NSA_DH), q_dtype),
                   jax.ShapeDtypeStruct((grp, t, 1024), f32),
                   jax.ShapeDtypeStruct((grp, t, 512), f32)),
        grid=(grp, t // tm),
        in_specs=[pl.BlockSpec((1, tm, 1024), lambda g_, i: (g_, i, C_NQ // 1024)),
                  pl.BlockSpec((1, tm, 1024), lambda g_, i: (g_, i, C_KVC // 1024)),
                  pl.BlockSpec((1, tm, 512), lambda g_, i: (g_, i, C_KVW // 512)),
                  tab, tab, tab,
                  pl.BlockSpec((1, LANES), lambda g_, i: (0, 0)),
                  pl.BlockSpec((3, LANES), lambda g_, i: (0, 0))],
        out_specs=(pl.BlockSpec((1, tm, 2 * NSA_HEADS * NSA_DH), lambda g_, i: (g_, i, 0)),
                   pl.BlockSpec((1, tm, 1024), lambda g_, i: (g_, i, 0)),
                   pl.BlockSpec((1, tm, 512), lambda g_, i: (g_, i, 0))),
        compiler_params=_cp(("arbitrary", "arbitrary")),
        name="nsa_prep",
    )(z, z, z, cos, sa, sb, gq, gk)


def _gla_kernel(*refs, chunk, sub, n_valid, has_init, hp):
    if has_init:
        q_ref, k_ref, v_ref, ga_ref, wa_ref, ba_ref, s0_ref, o_ref, so_ref, st_ref = refs
    else:
        q_ref, k_ref, v_ref, ga_ref, wa_ref, ba_ref, o_ref, so_ref, st_ref = refs
    i = pl.program_id(1)
    tb = q_ref.shape[1]
    ns = chunk // sub

    @pl.when(i == 0)
    def _():
        for h in range(GLA_HEADS):
            if has_init:
                st_ref[h] = s0_ref[0, h].T
            else:
                st_ref[h] = jnp.zeros((GLA_DV, GLA_DK), f32)

    ri = lax.broadcasted_iota(i32, (chunk, chunk), 0)
    ci = lax.broadcasted_iota(i32, (chunk, chunk), 1)
    ci_sub = lax.broadcasted_iota(i32, (sub, chunk), 1)
    tri =jnp.where(ri >= ci, 1.0, 0.0).astype(bf16)
    rrow = lax.broadcasted_iota(i32, (chunk, GLA_DK), 0)
    rmod = rrow % sub
    ga_off = C_GA % LANES

    def do_chunk(c, carry):
        r0 = pl.multiple_of(c * chunk, chunk)
        ga = ga_ref[0, pl.ds(r0, chunk), :][:, ga_off:ga_off + GLA_RANK]
        for h in range(GLA_HEADS):
            hs = slice(h * GLA_DK, (h + 1) * GLA_DK)
            q = q_ref[0, pl.ds(r0, chunk), hs] * (GLA_DK ** -0.5)
            k = k_ref[0, pl.ds(r0, chunk), hs]
            v = v_ref[0, pl.ds(r0, chunk), h * GLA_DV:(h + 1) * GLA_DV]
            x = _dotp(_dot, ga, wa_ref[:, hs], hp) + ba_ref[:, hs]
            la = (jnp.minimum(x, 0.0) - jnp.log1p(jnp.exp(-jnp.abs(x)))) * (1.0 / GLA_TAU)
            if n_valid < chunk:
                la = jnp.where(rrow < n_valid, la, 0.0)
                k = jnp.where(rrow < n_valid, k, 0.0)
            l1, l2, l3 = _split3(la)
            cum = _dot(tri, l1) + _dot(tri, l2) + _dot(tri, l3)
            parts = [jnp.zeros((sub, chunk), f32)]
            for sb_i in range(1, ns):
                lo = sb_i * sub
                m_i = cum[lo:lo + 1, :]
                q_i = q[lo:lo + sub] * jnp.exp(cum[lo:lo + sub] - m_i)
                k_i = k * jnp.exp(jnp.minimum(m_i - cum, 0.0))
                parts.append(jnp.where(ci_sub < lo, _dotp(_dot_nt, q_i, k_i, hp), 0.0))
            scores = parts[0] if ns == 1 else jnp.concatenate(parts, axis=0)
            for j in range(min(sub, n_valid)):
                kj = [jnp.broadcast_to(k[s * sub + j:s * sub + j + 1, :], (sub, GLA_DK)) for s in range(ns)]
                cj = [jnp.broadcast_to(cum[s * sub + j:s * sub + j + 1, :], (sub, GLA_DK)) for s in range(ns)]
                kj = kj[0] if ns == 1 else jnp.concatenate(kj, axis=0)
                cj = cj[0] if ns == 1 else jnp.concatenate(cj, axis=0)
                e = jnp.exp(jnp.where(rmod >= j, cum - cj, NEG_INF))
                val = jnp.sum(q * kj * e, axis=1, keepdims=True)
                scores = jnp.where(ci == (ri // sub) * sub + j, val, scores)
            st = st_ref[h]
            o = _dotp(_dot, scores, v, hp) + _dotp(_dot_nt, q * jnp.exp(cum), st, hp)
            o_ref[0, pl.ds(r0, chunk), h * GLA_DV:(h + 1) * GLA_DV] = o
            last = cum[chunk - 1:chunk, :]
            st_ref[h] = st * jnp.exp(last) + _dotp(_dot_tn, v, k * jnp.exp(last - cum), hp)
        return carry

    lax.fori_loop(0, tb // chunk, do_chunk, 0)

    @pl.when(i == pl.num_programs(1) - 1)
    def _():
        for h in range(GLA_HEADS):
            so_ref[0, h] = st_ref[h].T


def gla(z, wa, ba, s0, chunk, sub, n_valid, hp):
    b, t, _ = z.shape
    tb = min(t, 256)
    has_init = s0 is not None
    hk = GLA_HEADS * GLA_DK
    in_specs = [pl.BlockSpec((1, tb, hk), lambda b_, i: (b_, i, C_GQ // hk)),
                pl.BlockSpec((1, tb, hk), lambda b_, i: (b_, i, C_GK // hk)),
                pl.BlockSpec((1, tb, GLA_HEADS * GLA_DV), lambda b_, i: (b_, i, C_GV // (GLA_HEADS * GLA_DV))),
                pl.BlockSpec((1, tb, LANES), lambda b_, i: (b_, i, C_GA // LANES)),
                pl.BlockSpec((GLA_RANK, hk), lambda b_, i: (0, 0)),
                pl.BlockSpec((1, hk), lambda b_, i: (0, 0))]
    args = [z, z, z, z, wa, ba]
    if has_init:
        in_specs.append(pl.BlockSpec((1, GLA_HEADS, GLA_DK, GLA_DV), lambda b_, i: (b_, 0, 0, 0)))
        args.append(s0)
    return pl.pallas_call(
        functools.partial(_gla_kernel, chunk=chunk, sub=sub, n_valid=n_valid, has_init=has_init, hp=hp),
        out_shape=(jax.ShapeDtypeStruct((b, t, GLA_HEADS * GLA_DV), f32),
                   jax.ShapeDtypeStruct((b, GLA_HEADS, GLA_DK, GLA_DV), f32)),
        grid=(b, t // tb),
        in_specs=in_specs,
        out_specs=(pl.BlockSpec((1, tb, GLA_HEADS * GLA_DV), lambda b_, i: (b_, i, 0)),
                   pl.BlockSpec((1, GLA_HEADS, GLA_DK, GLA_DV), lambda b_, i: (b_, 0, 0, 0))),
        scratch_shapes=[pltpu.VMEM((GLA_HEADS, GLA_DV, GLA_DK), f32)],
        compiler_params=_cp(("arbitrary", "arbitrary")),
        name="gla",
    )(*args)


def _compress_core(x_refs, pe_ref, w1_refs, w2_refs, gk_ref, kc_ref, vc_ref, nblk):
    for c in range(4):
        br = c // 2

        def body(s, acc, c=c, br=br):
            rows = x_refs[c][pl.ds(s, nblk, stride=CMP_BLOCK), :] + pe_ref[br, pl.ds(s, 1), :]
            return acc + _mm(rows, [w[br, s] for w in w1_refs])

        acc = lax.fori_loop(0, CMP_BLOCK, body, jnp.zeros((nblk, 2 * LANES), f32))
        y = _mm(_silu(acc), [w[br] for w in w2_refs])
        if br == 0:
            kc_ref[0, :, (c % 2) * LANES:(c % 2 + 1) * LANES] = _rms64(y, gk_ref[...])
        else:
            vc_ref[0, :, (c % 2) * LANES:(c % 2 + 1) * LANES] = y


def _compress_prompt_kernel(x0, x1, x2, x3, pe_ref, w1_ref, w2_ref, gk_ref, kc_ref, vc_ref):
    nblk = x0.shape[1] // CMP_BLOCK
    _compress_core([x0.at[0], x1.at[0], x2.at[0], x3.at[0]], pe_ref, (w1_ref,), (w2_ref,), gk_ref, kc_ref, vc_ref, nblk)


def compress_prompt(kvc, pe2, w1c, w2c, gkc):
    b, t, _ = kvc.shape
    nblk = t // CMP_BLOCK
    cst = lambda shape: pl.BlockSpec(shape, lambda b_: (0,) * len(shape))
    xs = [pl.BlockSpec((1, t, LANES), functools.partial(lambda b_, c: (b_, 0, c), c=c)) for c in range(4)]
    return pl.pallas_call(
        _compress_prompt_kernel,
        out_shape=(jax.ShapeDtypeStruct((b, nblk, 256), f32), jax.ShapeDtypeStruct((b, nblk, 256), f32)),
        grid=(b,),
        in_specs=xs + [cst(pe2.shape), cst(w1c.shape), cst(w2c.shape), cst(gkc.shape)],
        out_specs=(pl.BlockSpec((1, nblk, 256), lambda b_: (b_, 0, 0)), pl.BlockSpec((1, nblk, 256), lambda b_: (b_, 0, 0))),
        compiler_params=_cp(("arbitrary",)),
        name="compress_prompt",
    )(kvc, kvc, kvc, kvc, pe2, w1c, w2c, gkc)


def _compress_sample_kernel(pt_ref, *refs, pages_macro):
    pages = refs[:PAGES_PER_STEP]
    pe_ref, w1h_ref, w1l_ref, w2h_ref, w2l_ref, gk_ref, kc_ref, vc_ref, x0, x1, x2, x3 = refs[PAGES_PER_STEP:]
    st = pl.program_id(2)
    xs = [x0, x1, x2, x3]
    psz = pages[0].shape[2]
    for kk in range(PAGES_PER_STEP):
        r0 = pl.multiple_of((st * PAGES_PER_STEP + kk) * psz, psz)
        for c in range(4):
            xs[c][pl.ds(r0, psz), :] = pages[kk][0, 0, :, c * LANES:(c + 1) * LANES]

    @pl.when(st == pl.num_programs(2) - 1)
    def _():
        _compress_core(xs, pe_ref, (w1h_ref, w1l_ref), (w2h_ref, w2l_ref), gk_ref, kc_ref, vc_ref,
                       pages_macro * psz // CMP_BLOCK)


def compress_sample(page_flat, cache4, layer, nb, n_pages, pe2, w1cs, w2cs, gkc):
    psz = cache4.shape[2]
    pages_macro = min(n_pages, 128)
    n_macro = n_pages // pages_macro
    steps = pages_macro // PAGES_PER_STEP
    nblk_macro = pages_macro * psz // CMP_BLOCK
    nblk = n_pages * psz // CMP_BLOCK

    def page_map(b_, m, s, pt, kk):
        return (layer, pt[b_ * n_pages + m * pages_macro + s * PAGES_PER_STEP + kk], 0, 0)

    cst = lambda shape: pl.BlockSpec(shape, lambda b_, m, s, pt: (0,) * len(shape), pipeline_mode=pl.Buffered(1))
    pspecs = [pl.BlockSpec((1, 1, psz, 512), functools.partial(page_map, kk=kk)) for kk in range(PAGES_PER_STEP)]
    ospec = pl.BlockSpec((1, nblk_macro, 256), lambda b_, m, s, pt: (b_, m, 0))
    return pl.pallas_call(
        functools.partial(_compress_sample_kernel, pages_macro=pages_macro),
        out_shape=(jax.ShapeDtypeStruct((nb, nblk, 256), f32), jax.ShapeDtypeStruct((nb, nblk, 256), f32)),
        grid_spec=pltpu.PrefetchScalarGridSpec(
            num_scalar_prefetch=1, grid=(nb, n_macro, steps),
            in_specs=pspecs + [cst(pe2.shape), cst(w1cs[0].shape), cst(w1cs[1].shape),
                               cst(w2cs[0].shape), cst(w2cs[1].shape), cst(gkc.shape)],
            out_specs=(ospec, ospec),
            scratch_shapes=[pltpu.VMEM((pages_macro * psz, LANES), f32) for _ in range(4)]),
        compiler_params=_cp(("arbitrary", "arbitrary", "arbitrary")),
        name="compress_sample",
    )(page_flat, *([cache4] * PAGES_PER_STEP), pe2, *w1cs, *w2cs, gkc)


def _stack_heads(qz, rows):
    return jnp.concatenate([qz[:, h * LANES:(h + 1) * LANES] for h in range(8)], axis=0)


def _unstack_heads(o, rows):
    lane = lax.broadcasted_iota(i32, (rows, LANES), 1)
    left = lane < NSA_DH
    outs = []
    for p in range(4):
        a = o[(2 * p) * rows:(2 * p + 1) * rows]
        b = o[(2 * p + 1) * rows:(2 * p + 2) * rows]
        if p < 2:
            outs.append(jnp.where(left, a, pltpu.roll(b, NSA_DH, 1)))
        else:
            outs.append(jnp.where(left, pltpu.roll(a, NSA_DH, 1), b))
    return jnp.concatenate(outs, axis=1)


def _rank_loop(sc_ref, nblk):
    shape = sc_ref.shape
    blk = lax.broadcasted_iota(i32, shape, 0)
    sc = sc_ref[...]

    def body(j, rank):
        rj = jnp.broadcast_to(sc_ref[pl.ds(j, 1), :], shape)
        beats = (rj > sc) | ((rj == sc) & (blk > j))
        return rank + jnp.where(beats, 1, 0)

    return lax.fori_loop(0, nblk, body, jnp.zeros(shape, i32))


def _cmpsel_prompt_kernel(qz_ref, kc_ref, vc_ref, oc_ref, sel_ref, sc_ref):
    qb = pl.program_id(1)
    tq = qz_ref.shape[1]
    nf = kc_ref.shape[1]
    tpos = qb * tq + lax.broadcasted_iota(i32, (8 * tq, nf), 0) % tq
    blk_end = (lax.broadcasted_iota(i32, (8 * tq, nf), 1) + 1) * CMP_BLOCK - 1
    vis = blk_end <= tpos
    for kp in range(2):
        q = _stack_heads(qz_ref[0, :, kp * 8 * LANES:(kp + 1) * 8 * LANES], tq)
        kc = kc_ref[0, :, kp * LANES:(kp + 1) * LANES].astype(bf16)
        vc = vc_ref[0, :, kp * LANES:(kp + 1) * LANES].astype(bf16)
        s = jnp.where(vis, _dot_nt(q, kc), NEG_INF)
        e = jnp.where(vis, jnp.exp(s - jnp.max(s, axis=1, keepdims=True)), 0.0)
        p = e / jnp.maximum(jnp.sum(e, axis=1, keepdims=True), 1e-30)
        oc_ref[0, :, kp * 4 * LANES:(kp + 1) * 4 * LANES] = _unstack_heads(_dot(p.astype(bf16), vc), tq)
        imp_a = p[0:tq] + p[tq:2 * tq] + p[2 * tq:3 * tq] + p[3 * tq:4 * tq]
        imp_b = p[4 * tq:5 * tq] + p[5 * tq:6 * tq] + p[6 * tq:7 * tq] + p[7 * tq:8 * tq]
        imp_t = jnp.concatenate([imp_a, imp_b], axis=1).T
        blk = lax.broadcasted_iota(i32, (nf, tq), 0)
        cur = (qb * tq + lax.broadcasted_iota(i32, (nf, tq), 1)) // CMP_BLOCK
        forced = (blk == 0) | (blk == cur) | (blk == cur - 1)
        sels = []
        for half in range(2):
            sc = jnp.where(blk <= cur, jnp.where(forced, FORCED_SCORE, imp_t[half * nf:(half + 1) * nf]), NEG_INF)
            sc_ref[...] = sc
            rank = _rank_loop(sc_ref, nf)
            sels.append(jnp.where((rank < TOP_N) & (sc > 0.5 * NEG_INF), 1.0, 0.0))
        sel_ref[0, :, kp * 2 * nf:(kp + 1) * 2 * nf] = jnp.concatenate(sels, axis=0).T.astype(bf16)


def cmpsel_prompt(qz, kc, vc):
    b, t, _ = qz.shape
    nf = kc.shape[1]
    tq = Q_BLOCK
    return pl.pallas_call(
        _cmpsel_prompt_kernel,
        out_shape=(jax.ShapeDtypeStruct((b, t, NSA_HEADS * NSA_DH), f32),
                   jax.ShapeDtypeStruct((b, t, NSA_KV_HEADS * nf), bf16)),
        grid=(b, t // tq),
        in_specs=[pl.BlockSpec((1, tq, 2 * NSA_HEADS * NSA_DH), lambda b_, i: (b_, i, 0)),
                  pl.BlockSpec((1, nf, 256), lambda b_, i: (b_, 0, 0)),
                  pl.BlockSpec((1, nf, 256), lambda b_, i: (b_, 0, 0))],
        out_specs=(pl.BlockSpec((1, tq, NSA_HEADS * NSA_DH), lambda b_, i: (b_, i, 0)),
                   pl.BlockSpec((1, tq, NSA_KV_HEADS * nf), lambda b_, i: (b_, i, 0))),
        scratch_shapes=[pltpu.VMEM((nf, tq), f32)],
        compiler_params=_cp(("arbitrary", "arbitrary")),
        name="cmpsel_prompt",
    )(qz, kc, vc)


def _attn_prompt_kernel(*refs, mode, kt):
    if mode == "slc":
        qz_ref, k_ref, v_ref, sel_ref, o_ref, m_ref, l_ref, acc_ref = refs
    else:
        qz_ref, k_ref, v_ref, o_ref, m_ref, l_ref, acc_ref = refs
    qb = pl.program_id(2)
    tq = qz_ref.shape[1]
    q = _stack_heads(qz_ref[0], tq)
    tpos = qb * tq + lax.broadcasted_iota(i32, (tq, kt), 0)
    kiota = lax.broadcasted_iota(i32, (tq, kt), 1)
    m_ref[...] = jnp.full(m_ref.shape, NEG_INF, f32)
    l_ref[...] = jnp.zeros(l_ref.shape, f32)
    acc_ref[...] = jnp.zeros(acc_ref.shape, f32)

    def tile(k0):
        k = k_ref[0, pl.ds(k0, kt), :].astype(bf16)
        v = v_ref[0, pl.ds(k0, kt), :].astype(bf16)
        kpos = k0 + kiota
        if mode == "slc":
            nf = sel_ref.shape[2] // 2
            er = lax.broadcasted_iota(i32, (2 * nf, kt), 0)
            ec = (k0 + lax.broadcasted_iota(i32, (2 * nf, kt), 1)) // CMP_BLOCK
            sel = sel_ref[0]
            ma = _dot(sel, jnp.where(er == ec, 1.0, 0.0).astype(bf16))
            mb = _dot(sel, jnp.where(er - nf == ec, 1.0, 0.0).astype(bf16))
            causal = kpos <= tpos
            biases = (jnp.where(causal & (ma > 0.5), 0.0, NEG_INF), jnp.where(causal & (mb > 0.5), 0.0, NEG_INF))
        else:
            ba = jnp.where((kpos <= tpos) & (kpos >= tpos - WINDOW), 0.0, NEG_INF)
            biases = (ba, ba)
        s_all = _dot_nt(q, k)
        heads = [s_all[h * tq:(h + 1) * tq] + biases[h // 4] for h in range(8)]
        m_cur = jnp.concatenate([jnp.broadcast_to(jnp.max(s, axis=1, keepdims=True), (tq, LANES)) for s in heads], axis=0)
        m_old = m_ref[...]
        m_new = jnp.maximum(m_old, m_cur)
        alpha = jnp.exp(m_old - m_new)
        m_ref[...] = m_new
        ps, sums = [], []
        for h in range(8):
            m_h = m_new[h * tq:(h + 1) * tq]
            p = jnp.exp(heads[h] - jnp.concatenate([m_h] * (kt // LANES), axis=1))
            sums.append(jnp.broadcast_to(jnp.sum(p, axis=1, keepdims=True), (tq, LANES)))
            ps.append(p.astype(bf16))
        l_ref[...] = alpha * l_ref[...] + jnp.concatenate(sums, axis=0)
        acc_ref[...] = alpha * acc_ref[...] + _dot(jnp.concatenate(ps, axis=0), v)

    if mode == "slc":
        def body(t_i, c):
            tile(pl.multiple_of(t_i * kt, kt))
            return c
        lax.fori_loop(0, (qb * tq + tq + kt - 1) // kt, body, 0)
    else:
        tile(pl.multiple_of(jnp.maximum(qb * tq - WINDOW, 0), tq))
    o_ref[0] = _unstack_heads(acc_ref[...] / l_ref[...], tq)


def attn_prompt(qz, kv, sel, mode):
    b, t, _ = qz.shape
    tq = Q_BLOCK
    if mode == "slc":
        kt, kcol, vcol = 512, 4, 6
    else:
        kt, kcol, vcol = WINDOW + tq, 0, 2
        assert t >= kt
    in_specs = [pl.BlockSpec((1, tq, 8 * LANES), lambda b_, kp, i: (b_, i, kp)),
                pl.BlockSpec((1, t, LANES), lambda b_, kp, i: (b_, 0, kcol + kp)),
                pl.BlockSpec((1, t, LANES), lambda b_, kp, i: (b_, 0, vcol + kp))]
    args = [qz, kv, kv]
    if mode == "slc":
        nf2 = sel.shape[2] // 2
        in_specs.append(pl.BlockSpec((1, tq, nf2), lambda b_, kp, i: (b_, i, kp)))
        args.append(sel)
    return pl.pallas_call(
        functools.partial(_attn_prompt_kernel, mode=mode, kt=kt),
        out_shape=jax.ShapeDtypeStruct((b, t, NSA_HEADS * NSA_DH), f32),
        grid=(b, 2, t // tq),
        in_specs=in_specs,
        out_specs=pl.BlockSpec((1, tq, 4 * LANES), lambda b_, kp, i: (b_, i, kp)),
        scratch_shapes=[pltpu.VMEM((8 * tq, LANES), f32)] * 3,
        compiler_params=_cp(("arbitrary", "arbitrary", "arbitrary")),
        name="attn_" + mode,
    )(*args)


def _cmpsel_sample_kernel(qz_ref, kc_ref, vc_ref, oc_ref, sel_ref, sc_ref, *, q_pos0):
    ts = qz_ref.shape[1]
    nf = kc_ref.shape[1]
    nfp = sc_ref.shape[0]
    rows = 8 * ts
    tpos = q_pos0 + lax.broadcasted_iota(i32, (rows, nf), 0) % ts
    vis = (lax.broadcasted_iota(i32, (rows, nf), 1) + 1) * CMP_BLOCK - 1 <= tpos
    for kp in range(2):
        q = _stack_heads(qz_ref[0, :, kp * 8 * LANES:(kp + 1) * 8 * LANES], ts)
        kc = kc_ref[0, :, kp * LANES:(kp + 1) * LANES]
        vc = vc_ref[0, :, kp * LANES:(kp + 1) * LANES]
        s = jnp.where(vis, _dotp(_dot_nt, q, kc, True), NEG_INF)
        e = jnp.where(vis, jnp.exp(s - jnp.max(s, axis=1, keepdims=True)), 0.0)
        p = e / jnp.maximum(jnp.sum(e, axis=1, keepdims=True), 1e-30)
        oc_ref[0, :, kp * 4 * LANES:(kp + 1) * 4 * LANES] = _unstack_heads(_dotp(_dot, p, vc, True), ts)
        imp_a = p[0:ts] + p[ts:2 * ts] + p[2 * ts:3 * ts] + p[3 * ts:4 * ts]
        imp_b = p[4 * ts:5 * ts] + p[5 * ts:6 * ts] + p[6 * ts:7 * ts] + p[7 * ts:8 * ts]
        imp = jnp.concatenate([imp_a] * 4 + [imp_b] * 4, axis=0)
        if nfp > nf:
            imp = jnp.concatenate([imp, jnp.zeros((rows, nfp - nf), f32)], axis=1)
        imp_t = jnp.concatenate([imp[:, c * LANES:(c + 1) * LANES].T for c in range(nfp // LANES)], axis=0)
        blk = lax.broadcasted_iota(i32, (nfp, rows), 0)
        cur = (q_pos0 + lax.broadcasted_iota(i32, (nfp, rows), 1) % ts) // CMP_BLOCK
        forced = (blk == 0) | (blk == cur) | (blk == cur - 1)
        sc = jnp.where(blk < nf, jnp.where(forced, FORCED_SCORE, imp_t), NEG_INF)
        sc_ref[...] = sc
        rank = _rank_loop(sc_ref, nf)
        sel_ref[0, kp] = jnp.where((rank < TOP_N - 1) & (sc > 0.5 * NEG_INF), 1.0, 0.0)


def cmpsel_sample(qz3, kc, vc, q_pos0):
    nb, ts, _ = qz3.shape
    nf = kc.shape[1]
    nfp = -(-nf // LANES) * LANES
    return pl.pallas_call(
        functools.partial(_cmpsel_sample_kernel, q_pos0=q_pos0),
        out_shape=(jax.ShapeDtypeStruct((nb, ts, NSA_HEADS * NSA_DH), f32),
                   jax.ShapeDtypeStruct((nb, 2, nfp, 8 * ts), f32)),
        grid=(nb,),
        in_specs=[pl.BlockSpec((1, ts, 2 * NSA_HEADS * NSA_DH), lambda b_: (b_, 0, 0)),
                  pl.BlockSpec((1, nf, 256), lambda b_: (b_, 0, 0)),
                  pl.BlockSpec((1, nf, 256), lambda b_: (b_, 0, 0))],
        out_specs=(pl.BlockSpec((1, ts, NSA_HEADS * NSA_DH), lambda b_: (b_, 0, 0)),
                   pl.BlockSpec((1, 2, nfp, 8 * ts), lambda b_: (b_, 0, 0, 0))),
        scratch_shapes=[pltpu.VMEM((nfp, 8 * ts), f32)],
        compiler_params=_cp(("arbitrary",)),
        name="cmpsel_sample",
    )(qz3, kc, vc)


def _slc_sample_kernel(pt_ref, *refs, n_new):
    npg = PAGES_PER_STEP
    kpages, vpages = refs[:npg], refs[npg:2 * npg]
    qz_ref, sel_ref, kn_ref, vn_ref, o_ref, m_ref, l_ref, acc_ref = refs[2 * npg:]
    st = pl.program_id(2)
    ts = qz_ref.shape[1]
    cols = 8 * ts
    q = _stack_heads(qz_ref[0], ts)

    @pl.when(st == 0)
    def _():
        m_ref[...] = jnp.full(m_ref.shape, NEG_INF, f32)
        l_ref[...] = jnp.zeros(l_ref.shape, f32)
        acc_ref[...] = jnp.zeros(acc_ref.shape, f32)

    def update(k, v, bias):
        s = _dotp(_dot_nt, k, q, True) + bias
        m_old = m_ref[...]
        m_new = jnp.maximum(m_old, jnp.max(s, axis=0, keepdims=True))
        p = jnp.exp(s - m_new)
        alpha = jnp.exp(m_old - m_new)
        l_ref[...] = alpha * l_ref[...] + jnp.sum(p, axis=0, keepdims=True)
        acc_ref[...] = alpha * acc_ref[...] + _dotp(_dot_tn, v, p, True)
        m_ref[...] = m_new

    k = jnp.concatenate([r[0, 0] for r in kpages], axis=0)
    v = jnp.concatenate([r[0, 0] for r in vpages], axis=0)
    psz = kpages[0].shape[2]
    bps = npg * psz // CMP_BLOCK
    sel = sel_ref[0, 0]
    er = lax.broadcasted_iota(i32, (bps * CMP_BLOCK, bps), 0) // CMP_BLOCK
    ec = lax.broadcasted_iota(i32, (bps * CMP_BLOCK, bps), 1)
    msk = _dot(jnp.where(er == ec, 1.0, 0.0).astype(bf16), sel.astype(bf16))
    bias = jnp.where(msk > 0.5, 0.0, NEG_INF)
    update(k, v, bias)

    @pl.when(st == pl.num_programs(2) - 1)
    def _():
        kn = kn_ref[0]
        vn = vn_ref[0]
        j = lax.broadcasted_iota(i32, (ts, cols), 0)
        tok = lax.broadcasted_iota(i32, (ts, cols), 1) % ts
        update(kn, vn, jnp.where((j <= tok) & (j < n_new), 0.0, NEG_INF))
        o = (acc_ref[...] / l_ref[...]).T
        o_ref[0] = _unstack_heads(o, ts)


def slc_sample(page_flat, cache4, layer, n_pages, qz3, sel_t, kvc3, n_new):
    nb, ts, _ = qz3.shape
    psz = cache4.shape[2]
    steps = n_pages // PAGES_PER_STEP
    bps = PAGES_PER_STEP * psz // CMP_BLOCK
    cols = 8 * ts

    def page_map(b_, kp, s, pt, kk, col):
        return (layer, pt[b_ * n_pages + s * PAGES_PER_STEP + kk], 0, col + kp)

    kspecs = [pl.BlockSpec((1, 1, psz, LANES), functools.partial(page_map, kk=kk, col=4)) for kk in range(PAGES_PER_STEP)]
    vspecs = [pl.BlockSpec((1, 1, psz, LANES), functools.partial(page_map, kk=kk, col=6)) for kk in range(PAGES_PER_STEP)]
    return pl.pallas_call(
        functools.partial(_slc_sample_kernel, n_new=n_new),
        out_shape=jax.ShapeDtypeStruct((nb, ts, NSA_HEADS * NSA_DH), f32),
        grid_spec=pltpu.PrefetchScalarGridSpec(
            num_scalar_prefetch=1, grid=(nb, 2, steps),
            in_specs=kspecs + vspecs + [
                pl.BlockSpec((1, ts, 8 * LANES), lambda b_, kp, s, pt: (b_, 0, kp)),
                pl.BlockSpec((1, 1, bps, cols), lambda b_, kp, s, pt: (b_, kp, s, 0)),
                pl.BlockSpec((1, ts, LANES), lambda b_, kp, s, pt: (b_, 0, 4 + kp)),
                pl.BlockSpec((1, ts, LANES), lambda b_, kp, s, pt: (b_, 0, 6 + kp))],
            out_specs=pl.BlockSpec((1, ts, 4 * LANES), lambda b_, kp, s, pt: (b_, 0, kp)),
            scratch_shapes=[pltpu.VMEM((1, cols), f32), pltpu.VMEM((1, cols), f32), pltpu.VMEM((LANES, cols), f32)]),
        compiler_params=_cp(("arbitrary", "arbitrary", "arbitrary")),
        name="slc_sample",
    )(page_flat, *([cache4] * (2 * PAGES_PER_STEP)), qz3, sel_t, kvc3, kvc3)


def _win_sample_kernel(qz_ref, old_ref, new_ref, o_ref, nw_ref, *, n_new):
    ts = qz_ref.shape[1]
    wb = old_ref.shape[2]
    rows = 8 * ts
    old = old_ref[0, 0]
    new = new_ref[0]
    tok_o = lax.broadcasted_iota(i32, (rows, wb), 0) % ts
    j_o = lax.broadcasted_iota(i32, (rows, wb), 1)
    bias_o = jnp.where((j_o >= tok_o + (wb - WINDOW)), 0.0, NEG_INF)
    tok_n = lax.broadcasted_iota(i32, (rows, ts), 0) % ts
    j_n = lax.broadcasted_iota(i32, (rows, ts), 1)
    bias_n = jnp.where((j_n <= tok_n) & (j_n < n_new), 0.0, NEG_INF)
    for kp in range(2):
        q = _stack_heads(qz_ref[0, :, kp * 8 * LANES:(kp + 1) * 8 * LANES], ts)
        ko = old[:, kp * LANES:(kp + 1) * LANES]
        vo = old[:, 256 + kp * LANES:256 + (kp + 1) * LANES]
        kn = new[:, kp * LANES:(kp + 1) * LANES]
        vn = new[:, 256 + kp * LANES:256 + (kp + 1) * LANES]
        so = _dotp(_dot_nt, q, ko, True) + bias_o
        sn = _dotp(_dot_nt, q, kn, True) + bias_n
        m = jnp.maximum(jnp.max(so, axis=1, keepdims=True), jnp.max(sn, axis=1, keepdims=True))
        po, pn = jnp.exp(so - m), jnp.exp(sn - m)
        den = jnp.sum(po, axis=1, keepdims=True) + jnp.sum(pn, axis=1, keepdims=True)
        o = (_dotp(_dot, po, vo, True) + _dotp(_dot, pn, vn, True)) / den
        o_ref[0, :, kp * 4 * LANES:(kp + 1) * 4 * LANES] = _unstack_heads(o, ts)
    both = jnp.concatenate([old, new], axis=0)
    nw_ref[0, 0] = pltpu.roll(both, wb + ts - n_new, 0)[0:wb]


def win_sample(qz3, state_win4, layer, kvw3, n_new):
    nb, ts, _ = qz3.shape
    wb = state_win4.shape[2]
    return pl.pallas_call(
        functools.partial(_win_sample_kernel, n_new=n_new),
        out_shape=(jax.ShapeDtypeStruct((nb, ts, NSA_HEADS * NSA_DH), f32),
                   jax.ShapeDtypeStruct((1, nb, wb, 512), f32)),
        grid=(nb,),
        in_specs=[pl.BlockSpec((1, ts, 2 * NSA_HEADS * NSA_DH), lambda b_: (b_, 0, 0)),
                  pl.BlockSpec((1, 1, wb, 512), lambda b_: (layer, b_, 0, 0)),
                  pl.BlockSpec((1, ts, 512), lambda b_: (b_, 0, 0))],
        out_specs=(pl.BlockSpec((1, ts, NSA_HEADS * NSA_DH), lambda b_: (b_, 0, 0)),
                   pl.BlockSpec((1, 1, wb, 512), lambda b_: (0, b_, 0, 0))),
        compiler_params=_cp(("arbitrary",)),
        name="win_sample",
    )(qz3, state_win4, kvw3)


def _mix1_kernel(*refs, nw):
    og_ref, r_ref, ng_ref, mg0_ref, mg1_ref, oc_ref, os_ref, ow_ref, gg_ref = refs[:9]
    wa_refs, wb_refs = refs[9:9 + nw], refs[9 + nw:9 + 2 * nw]
    o_ref = refs[9 + 2 * nw]
    a_refs, b_refs = refs[10 + 2 * nw:10 + 3 * nw], refs[10 + 3 * nw:]
    j = pl.program_id(2)

    @pl.when(j == 0)
    def _():
        ys = []
        for h in range(GLA_HEADS):
            hs = slice(h * GLA_DV, (h + 1) * GLA_DV)
            x = og_ref[0, :, hs]
            y = x * lax.rsqrt(jnp.mean(x * x, axis=-1, keepdims=True) + EPS) * gg_ref[:, hs]
            ys.append(y * _silu(r_ref[0, :, hs]))
        _store_split(jnp.concatenate(ys, axis=1), a_refs)
        ghi, glo = _split2(jax.nn.sigmoid(ng_ref[0]))
        nh = NSA_HEADS * NSA_DH
        er = lax.broadcasted_iota(i32, (LANES, nh), 0)
        ec = lax.broadcasted_iota(i32, (LANES, nh), 1) // NSA_DH
        acc = None
        for br, ref in enumerate((oc_ref, os_ref, ow_ref)):
            ex = jnp.where(er == br * NSA_HEADS + ec, 1.0, 0.0).astype(bf16)
            term = (_dot(ghi, ex) + _dot(glo, ex)) * ref[0]
            acc = term if acc is None else acc + term
        _store_split(acc, b_refs)

    o_ref[0] = (mg0_ref[0] * _mm_split(a_refs, wa_refs) + mg1_ref[0] * _mm_split(b_refs, wb_refs)).astype(o_ref.dtype)


def mix1(o_gla, z, o_cmp, o_slc, o_win, gg, was, wbs):
    grp, t, n1 = o_gla.shape
    d = was[0].shape[1]
    nw = len(was)
    tm, tn = min(t, 512), 1024
    row = lambda w, c: pl.BlockSpec((1, tm, w), functools.partial(lambda g_, i, j, c: (g_, i, c), c=c))
    return pl.pallas_call(
        functools.partial(_mix1_kernel, nw=nw),
        out_shape=jax.ShapeDtypeStruct((grp, t, d), bf16 if nw == 1 else f32),
        grid=(grp, t // tm, d // tn),
        in_specs=[row(n1, 0), row(1024, C_GR // 1024), row(LANES, C_NG // LANES),
                  pl.BlockSpec((1, tm, tn), lambda g_, i, j: (g_, i, C_MG // tn + j)),
                  pl.BlockSpec((1, tm, tn), lambda g_, i, j: (g_, i, (C_MG + d) // tn + j)),
                  row(n1, 0), row(n1, 0), row(n1, 0),
                  pl.BlockSpec((1, n1), lambda g_, i, j: (0, 0))]
        + [pl.BlockSpec((n1, tn), lambda g_, i, j: (0, j))] * (2 * nw),
        out_specs=pl.BlockSpec((1, tm, tn), lambda g_, i, j: (g_, i, j)),
        scratch_shapes=[pltpu.VMEM((tm, n1), bf16)] * (2 * nw),
        compiler_params=_cp(("arbitrary", "arbitrary", "arbitrary")),
        name="mix1",
    )(o_gla, z, z, z, z, o_cmp, o_slc, o_win, gg, *was, *wbs)


def _mm_res_kernel(*refs, nw):
    a_ref = refs[0]
    w_refs = refs[1:1 + nw]
    x_ref, gt_ref, o_ref = refs[1 + nw:]
    o_ref[0] = x_ref[0] + gt_ref[0] * _mm(a_ref[0], [w[...] for w in w_refs])


def mm_res(a, ws, x, gate):
    grp, t, k = a.shape
    d = ws[0].shape[1]
    nw = len(ws)
    tm, tn = min(t, 1024), 512
    per_row = gate.shape[1] != 1
    return pl.pallas_call(
        functools.partial(_mm_res_kernel, nw=nw),
        out_shape=jax.ShapeDtypeStruct((grp, t, d), f32),
        grid=(grp, t // tm, d // tn),
        in_specs=[pl.BlockSpec((1, tm, k), lambda g_, i, j: (g_, i, 0))]
        + [pl.BlockSpec((k, tn), lambda g_, i, j: (0, j))] * nw
        + [pl.BlockSpec((1, tm, tn), lambda g_, i, j: (g_, i, j)),
           pl.BlockSpec((1, tm, tn), lambda g_, i, j: (g_, i, j)) if per_row
           else pl.BlockSpec((1, 1, tn), lambda g_, i, j: (g_, 0, j))],
        out_specs=pl.BlockSpec((1, tm, tn), lambda g_, i, j: (g_, i, j)),
        compiler_params=_cp(("arbitrary", "arbitrary", "arbitrary")),
        name="mm_res",
    )(a, *ws, x, gate)


def _ffn1_kernel(*refs, nw):
    x_ref, g_ref, sh_ref, sc_ref = refs[:4]
    wg_refs, wu_refs = refs[4:4 + nw], refs[4 + nw:4 + 2 * nw]
    o_ref = refs[4 + 2 * nw]
    h_refs = refs[5 + 2 * nw:]

    @pl.when(pl.program_id(2) == 0)
    def _():
        _store_split(_modulated(x_ref, g_ref, sh_ref, sc_ref), h_refs)

    o_ref[0] = (_silu(_mm_split(h_refs, wg_refs)) * _mm_split(h_refs, wu_refs)).astype(o_ref.dtype)


def ffn1(x, g, shift, scale, wgs, wus):
    grp, t, d = x.shape
    f = wgs[0].shape[1]
    nw = len(wgs)
    tm, tn = min(t, 1024), 512
    per_row = shift.shape[1] != 1
    return pl.pallas_call(
        functools.partial(_ffn1_kernel, nw=nw),
        out_shape=jax.ShapeDtypeStruct((grp, t, f), bf16 if nw == 1 else f32),
        grid=(grp, t // tm, f // tn),
        in_specs=[pl.BlockSpec((1, tm, d), lambda g_, i, j: (g_, i, 0)),
                  pl.BlockSpec((1, d), lambda g_, i, j: (0, 0)),
                  _mod_specs(t, tm, d, per_row), _mod_specs(t, tm, d, per_row)]
        + [pl.BlockSpec((d, tn), lambda g_, i, j: (0, j))] * (2 * nw),
        out_specs=pl.BlockSpec((1, tm, tn), lambda g_, i, j: (g_, i, j)),
        scratch_shapes=[pltpu.VMEM((tm, d), bf16)] * nw,
        compiler_params=_cp(("arbitrary", "arbitrary", "arbitrary")),
        name="ffn1",
    )(x, g, shift, scale, *wgs, *wus)


def _router_kernel(x_ref, g_ref, sh_ref, sc_ref, whi_ref, wlo_ref, h_ref, r_ref):
    h = _modulated(x_ref, g_ref, sh_ref, sc_ref)
    h_ref[0] = h
    hi, lo = _split2(h)
    logits = _dot(hi, whi_ref[...]) + _dot(lo, whi_ref[...]) + _dot(hi, wlo_ref[...])
    lane = lax.broadcasted_iota(i32, logits.shape, 1)
    lg = jnp.where(lane < N_EXPERTS, logits, -jnp.inf)
    v1 = jnp.max(lg, axis=1, keepdims=True)
    i1 = jnp.min(jnp.where(lg == v1, lane, LANES), axis=1, keepdims=True)
    lg2 = jnp.where(lane == i1, -jnp.inf, lg)
    v2 = jnp.max(lg2, axis=1, keepdims=True)
    i2 = jnp.min(jnp.where(lg2 == v2, lane, LANES), axis=1, keepdims=True)
    e2 = jnp.exp(v2 - v1)
    g1 = 1.0 / (1.0 + e2)
    g2 = e2 / (1.0 + e2)
    r_ref[0] = jnp.where(lane == 0, i1.astype(f32), jnp.where(lane == 1, i2.astype(f32),
                         jnp.where(lane == 2, g1, jnp.where(lane == 3, g2, 0.0))))


def router(x, g, shift, scale, whi, wlo):
    grp, t, d = x.shape
    tm = min(t, 512)
    per_row = shift.shape[1] != 1
    mod = (pl.BlockSpec((1, tm, d), lambda g_, i: (g_, i, 0)) if per_row
           else pl.BlockSpec((1, 1, d), lambda g_, i: (g_, 0, 0)))
    return pl.pallas_call(
        _router_kernel,
        out_shape=(jax.ShapeDtypeStruct((grp, t, d), f32), jax.ShapeDtypeStruct((grp, t, LANES), f32)),
        grid=(grp, t // tm),
        in_specs=[pl.BlockSpec((1, tm, d), lambda g_, i: (g_, i, 0)),
                  pl.BlockSpec((1, d), lambda g_, i: (0, 0)), mod, mod,
                  pl.BlockSpec((d, LANES), lambda g_, i: (0, 0)),
                  pl.BlockSpec((d, LANES), lambda g_, i: (0, 0))],
        out_specs=(pl.BlockSpec((1, tm, d), lambda g_, i: (g_, i, 0)),
                   pl.BlockSpec((1, tm, LANES), lambda g_, i: (g_, i, 0))),
        compiler_params=_cp(("arbitrary", "arbitrary")),
        name="router",
    )(x, g, shift, scale, whi, wlo)


def _gather_rows_kernel(idx_ref, src_ref, dst_ref, sem, *, rows):
    base = pl.program_id(0) * rows

    def row_copy(src_row, dst_row):
        return pltpu.make_async_copy(src_ref.at[pl.ds(src_row, 1)], dst_ref.at[pl.ds(dst_row, 1)], sem)

    def start(r, c):
        row_copy(idx_ref[base + r], r).start()
        return c

    def wait(r, c):
        row_copy(0, r).wait()
        return c

    lax.fori_loop(0, rows, start, 0, unroll=8)
    lax.fori_loop(0, rows, wait, 0, unroll=8)


def gather_rows(idx, src, n_out):
    rows = 512 if n_out % 512 == 0 else 128
    return pl.pallas_call(
        functools.partial(_gather_rows_kernel, rows=rows),
        out_shape=jax.ShapeDtypeStruct((n_out, src.shape[1]), src.dtype),
        grid_spec=pltpu.PrefetchScalarGridSpec(
            num_scalar_prefetch=1, grid=(n_out // rows,),
            in_specs=[pl.BlockSpec(memory_space=pl.ANY)],
            out_specs=pl.BlockSpec((rows, src.shape[1]), lambda i, idx_: (i, 0)),
            scratch_shapes=[pltpu.SemaphoreType.DMA]),
        compiler_params=_cp(("arbitrary",)),
        name="gather_rows",
    )(idx, src)


def _moe1_kernel(te_ref, nt_ref, xs_ref, *refs, nw):
    wg_refs, wu_refs, o_ref = refs[:nw], refs[nw:2 * nw], refs[2 * nw]

    @pl.when(pl.program_id(1) < nt_ref[0])
    def _():
        x = xs_ref[...]
        gate = _mm(x, [w[0] for w in wg_refs])
        o_ref[...] = (_silu(gate) * _mm(x, [w[0] for w in wu_refs])).astype(o_ref.dtype)

    @pl.when(pl.program_id(1) >= nt_ref[0])
    def _():
        o_ref[...] = jnp.zeros(o_ref.shape, o_ref.dtype)


def moe1(tile_expert, n_tiles_used, xs, wgs, wus, tm):
    p, d = xs.shape
    fe = wgs[0].shape[2]
    nw = len(wgs)
    tn = fe // 2 if nw == 1 else 256
    return pl.pallas_call(
        functools.partial(_moe1_kernel, nw=nw),
        out_shape=jax.ShapeDtypeStruct((p, fe), bf16 if nw == 1 else f32),
        grid_spec=pltpu.PrefetchScalarGridSpec(
            num_scalar_prefetch=2, grid=(fe // tn, p // tm),
            in_specs=[pl.BlockSpec((tm, d), lambda j, i, te, nt: (i, 0))]
            + [pl.BlockSpec((1, d, tn), lambda j, i, te, nt: (te[i], 0, j))] * (2 * nw),
            out_specs=pl.BlockSpec((tm, tn), lambda j, i, te, nt: (i, j))),
        compiler_params=_cp(("arbitrary", "arbitrary")),
        name="moe1",
    )(tile_expert, n_tiles_used, xs, *wgs, *wus)


def _moe2_kernel(te_ref, nt_ref, a_ref, *refs, nw):
    wd_refs, sg_ref, o_ref = refs[:nw], refs[nw], refs[nw + 1]

    @pl.when(pl.program_id(1) < nt_ref[0])
    def _():
        o_ref[...] = sg_ref[...] * _mm(a_ref[...], [w[0] for w in wd_refs])

    @pl.when(pl.program_id(1) >= nt_ref[0])
    def _():
        o_ref[...] = jnp.zeros(o_ref.shape, o_ref.dtype)


def moe2(tile_expert, n_tiles_used, act, wds, slot_gate, tm):
    p, fe = act.shape
    d = wds[0].shape[2]
    nw = len(wds)
    tn = d // 2 if nw == 1 else 512
    return pl.pallas_call(
        functools.partial(_moe2_kernel, nw=nw),
        out_shape=jax.ShapeDtypeStruct((p, d), f32),
        grid_spec=pltpu.PrefetchScalarGridSpec(
            num_scalar_prefetch=2, grid=(d // tn, p // tm),
            in_specs=[pl.BlockSpec((tm, fe), lambda j, i, te, nt: (i, 0))]
            + [pl.BlockSpec((1, fe, tn), lambda j, i, te, nt: (te[i], 0, j))] * nw
            + [pl.BlockSpec((tm, 1), lambda j, i, te, nt: (i, 0))],
            out_specs=pl.BlockSpec((tm, tn), lambda j, i, te, nt: (i, j))),
        compiler_params=_cp(("arbitrary", "arbitrary")),
        name="moe2",
    )(tile_expert, n_tiles_used, act, *wds, slot_gate)


def _combine_kernel(x_ref, gt_ref, y1_ref, y2_ref, o_ref):
    o_ref[0] = x_ref[0] + gt_ref[0] * (y1_ref[0] + y2_ref[0])


def combine(x, gate, y1, y2):
    grp, t, d = x.shape
    tm = min(t, 512)
    per_row = gate.shape[1] != 1
    row = pl.BlockSpec((1, tm, d), lambda g_, i: (g_, i, 0))
    return pl.pallas_call(
        _combine_kernel,
        out_shape=jax.ShapeDtypeStruct((grp, t, d), f32),
        grid=(grp, t // tm),
        in_specs=[row, row if per_row else pl.BlockSpec((1, 1, d), lambda g_, i: (g_, 0, 0)), row, row],
        out_specs=row,
        compiler_params=_cp(("arbitrary", "arbitrary")),
        name="combine",
    )(x, gate, y1, y2)


def moe_layer(x, g, shift, scale, gate, whi, wlo, wgs, wus, wds):
    grp, t, d = x.shape
    n = grp * t
    tm = 512 if n >= 4096 else 128
    h, route = router(x, g, shift, scale, whi, wlo)
    route = route.reshape(n, LANES)
    e_idx = route[:, 0:TOP_K].astype(i32)
    gates = route[:, 2:2 + TOP_K]
    e_flat = e_idx.reshape(-1)
    onehot = (e_flat[:, None] == jnp.arange(N_EXPERTS, dtype=i32)[None, :]).astype(i32)
    within = jnp.cumsum(onehot, axis=0) - onehot
    counts = jnp.sum(onehot, axis=0)
    padded = (counts + tm - 1) // tm * tm
    starts = jnp.cumsum(padded) - padded
    slot = jnp.sum(onehot * (starts[None, :] + within), axis=1)
    p = TOP_K * n + N_EXPERTS * tm
    p = -(-p // 256) * 256
    slot_tok = jnp.zeros((p,), i32).at[slot].set(jnp.arange(TOP_K * n, dtype=i32) // TOP_K)
    slot_gate = jnp.zeros((p,), f32).at[slot].set(gates.reshape(-1))
    n_tiles = p // tm
    ends = jnp.cumsum(padded)
    tile_start = jnp.arange(n_tiles, dtype=i32) * tm
    tile_expert = jnp.minimum(jnp.sum((tile_start[:, None] >= ends[None, :]).astype(i32), axis=1), N_EXPERTS - 1).astype(i32)
    n_used = (ends[-1] // tm).astype(i32).reshape(1)
    xs = gather_rows(slot_tok, h.reshape(n, d), p)
    act = moe1(tile_expert, n_used, xs, wgs, wus, tm)
    ys = moe2(tile_expert, n_used, act, wds, slot_gate.reshape(p, 1), tm)
    slots = slot.reshape(n, TOP_K)
    y1 = gather_rows(slots[:, 0], ys, n).reshape(grp, t, d)
    y2 = gather_rows(slots[:, 1], ys, n).reshape(grp, t, d)
    return combine(x, gate, y1, y2)


def _rope_tables(pos):
    half = ROPE_DIMS // 2
    inv = ROPE_THETA ** (-(jnp.arange(half, dtype=f32) * 2.0 / ROPE_DIMS))
    ang = pos.astype(f32)[:, None] * inv[None, :]
    cos, sin = jnp.cos(ang), jnp.sin(ang)
    n = pos.shape[0]
    pad = jnp.zeros((n, NSA_DH - ROPE_DIMS), f32)
    zero = jnp.zeros((n, half), f32)
    c64 = jnp.concatenate([cos, cos, pad + 1.0], axis=1)
    a64 = jnp.concatenate([-sin, zero, pad], axis=1)
    b64 = jnp.concatenate([zero, sin, pad], axis=1)
    return tuple(jnp.tile(x, (1, 2)) for x in (c64, a64, b64))


def _hilo(w):
    hi = lax.reduce_precision(w, exponent_bits=8, mantissa_bits=7)
    return hi.astype(bf16), (w - hi).astype(bf16)


def _pair_diag(w):
    z = jnp.zeros_like(w)
    return jnp.concatenate([jnp.concatenate([w, z], axis=-1), jnp.concatenate([z, w], axis=-1)], axis=-2)


def _mods(mod_l, rows, per_row_t):
    d = mod_l.shape[1] // 6
    m = mod_l.reshape(rows, 2, 3, d)
    out = {}
    for si, sname in enumerate(("mix", "ffn")):
        for ti, tname in enumerate(("shift", "scale", "gate")):
            v = m[:, si, ti]
            if per_row_t:
                v = jnp.repeat(v, per_row_t, axis=0)[None]
            else:
                v = v[:, None, :]
            out[sname + "_" + tname] = v
    return out


def kernel(x_prompt, x_sample, cache_kv, state_kv_win, state_gla, page_table, c_prompt, c_sample, norm_g, w_ada, b_ada, w_in, w_alpha2, b_alpha, gla_norm_g, qk_g, phi_pe, phi_w1, phi_w2, w_branch_a, w_branch_b, w_out, w_ffn_gate, w_ffn_up, w_ffn_down, w_router, w_exp_gate, w_exp_up, w_exp_down):
    bp, tp, d = x_prompt.shape
    bs, t_new, _ = x_sample.shape
    depth = w_in.shape[0]
    n_pool, psz = cache_kv.shape[1], cache_kv.shape[2]
    n_pages = page_table.shape[1]
    past_len = n_pages * psz
    wbuf = state_kv_win.shape[2]
    ts = TS_PAD

    w_in_r = _hilo(jnp.concatenate([
        w_in[:, :, 0:2048], w_in[:, :, 2064:3088], w_in[:, :, 3088:4112], w_in[:, :, 4112:5648],
        w_in[:, :, 5648:5696], w_in[:, :, 2048:2064], jnp.zeros((depth, d, C_MG - C_GA - GLA_RANK), f32),
        w_in[:, :, 5696:9792]], axis=2))
    ba2 = b_alpha.reshape(depth, 1, -1)
    wba, wbb, wo = _hilo(w_branch_a), _hilo(w_branch_b), _hilo(w_out)
    wfg, wfu, wfd = _hilo(w_ffn_gate), _hilo(w_ffn_up), _hilo(w_ffn_down)
    weg, weu, wed = _hilo(w_exp_gate), _hilo(w_exp_up), _hilo(w_exp_down)
    wr_hi, wr_lo = _hilo(jnp.pad(w_router, ((0, 0), (0, 0), (0, LANES - N_EXPERTS))))
    w1c = _hilo(_pair_diag(phi_w1.reshape(depth, 2, CMP_BLOCK, NSA_DH, -1)))
    w2c = _hilo(_pair_diag(phi_w2))
    pe2 = jnp.tile(phi_pe, (1, 1, 1, 2))
    gq = jnp.tile(qk_g[:, 0:1, :], (1, 1, 2))
    gk = jnp.tile(qk_g[:, 1:4, :], (1, 1, 2))
    gkc = jnp.tile(qk_g[:, 4:5, :], (1, 1, 2))
    ggl = gla_norm_g.reshape(depth, 1, -1)
    ng = norm_g.reshape(depth, 2, 1, d)

    c_pad = jnp.zeros((16, d), f32).at[:bp].set(c_prompt).at[bp:bp + bs].set(c_sample)
    mod_all = ada_all(c_pad, w_ada, b_ada)

    cos_p, sa_p, sb_p = _rope_tables(jnp.arange(tp, dtype=i32))
    pos_s = past_len + (jnp.arange(bs * ts, dtype=i32) % ts)
    cos_s, sa_s, sb_s = _rope_tables(pos_s)

    page_flat = page_table.reshape(-1).astype(i32)
    cache4 = cache_kv.reshape(depth, n_pool, psz, 4 * NSA_KV_HEADS * NSA_DH)
    win4 = state_kv_win.reshape(depth, bs, wbuf, 2 * NSA_KV_HEADS * NSA_DH)

    x_p = x_prompt
    x_s = jnp.zeros((bs, ts, d), f32).at[:, :t_new].set(x_sample).reshape(1, bs * ts, d)
    kv_p, win_p, gla_p, kv_s, win_s, gla_s = [], [], [], [], [], []
    hi = lambda w, i: (w[0][i],)
    hl = lambda w, i: (w[0][i], w[1][i])
    flat = lambda a: a.reshape(1, bs * ts, -1)
    for l in range(depth):
        mp = _mods(mod_all[l, :bp], bp, 0)
        ms = _mods(mod_all[l, bp:bp + bs], bs, ts)
        z = proj(x_p, ng[l, 0], mp["mix_shift"], mp["mix_scale"], hi(w_in_r, l))
        qz, kvc, kvw = nsa_prep(z, cos_p, sa_p, sb_p, gq[l], gk[l], bf16)
        o_gla, st = gla(z, w_alpha2[l], ba2[l], None, GLA_CHUNK, GLA_SUB, GLA_CHUNK, False)
        kc, vc = compress_prompt(kvc, pe2[l], w1c[0][l], w2c[0][l], gkc[l])
        o_cmp, sel = cmpsel_prompt(qz, kc, vc)
        o_slc = attn_prompt(qz, kvc, sel, "slc")
        o_win = attn_prompt(qz, kvw, None, "win")
        merged = mix1(o_gla, z, o_cmp, o_slc, o_win, ggl[l], hi(wba, l), hi(wbb, l))
        x_p = mm_res(merged, hi(wo, l), x_p, mp["mix_gate"])
        kv_p.append(kvc.reshape(bp, tp, 4, NSA_KV_HEADS, NSA_DH))
        win_p.append(kvw[:, tp - WINDOW:].reshape(bp, WINDOW, 2, NSA_KV_HEADS, NSA_DH))
        gla_p.append(st)
        z = proj(x_s, ng[l, 0], ms["mix_shift"], ms["mix_scale"], hl(w_in_r, l))
        qz, kvc, kvw = nsa_prep(z, cos_s, sa_s, sb_s, gq[l], gk[l], f32)
        z3 = z.reshape(bs, ts, N_PROJ)
        o_gla, st = gla(z3, w_alpha2[l], ba2[l], state_gla[l], ts, ts, t_new, True)
        qz3 = qz.reshape(bs, ts, -1)
        kvc3 = kvc.reshape(bs, ts, -1)
        kvw3 = kvw.reshape(bs, ts, -1)
        kc, vc = compress_sample(page_flat, cache4, l, bs, n_pages, pe2[l], hl(w1c, l), hl(w2c, l), gkc[l])
        o_cmp, sel_t = cmpsel_sample(qz3, kc, vc, past_len)
        o_slc = slc_sample(page_flat, cache4, l, n_pages, qz3, sel_t, kvc3, t_new)
        o_win, new_win = win_sample(qz3, win4, l, kvw3, t_new)
        merged = mix1(flat(o_gla), z, flat(o_cmp), flat(o_slc), flat(o_win), ggl[l], hl(wba, l), hl(wbb, l))
        x_s = mm_res(merged, hl(wo, l), x_s, ms["mix_gate"])
        kv_s.append(kvc3[:, :t_new].reshape(bs, t_new, 4, NSA_KV_HEADS, NSA_DH))
        win_s.append(new_win.reshape(bs, wbuf, 2, NSA_KV_HEADS, NSA_DH))
        gla_s.append(st)
        e = l // 2
        if l % 2 == 0:
            a = ffn1(x_p, ng[l, 1], mp["ffn_shift"], mp["ffn_scale"], hi(wfg, e), hi(wfu, e))
            x_p = mm_res(a, hi(wfd, e), x_p, mp["ffn_gate"])
            a = ffn1(x_s, ng[l, 1], ms["ffn_shift"], ms["ffn_scale"], hl(wfg, e), hl(wfu, e))
            x_s = mm_res(a, hl(wfd, e), x_s, ms["ffn_gate"])
        else:
            x_p = moe_layer(x_p, ng[l, 1], mp["ffn_shift"], mp["ffn_scale"], mp["ffn_gate"],
                            wr_hi[e], wr_lo[e], hi(weg, e), hi(weu, e), hi(wed, e))
            x_s = moe_layer(x_s, ng[l, 1], ms["ffn_shift"], ms["ffn_scale"], ms["ffn_gate"],
                            wr_hi[e], wr_lo[e], hl(weg, e), hl(weu, e), hl(wed, e))
    y_s = x_s.reshape(bs, ts, d)[:, :t_new]
    return (x_p, y_s, jnp.stack(kv_p), jnp.stack(win_p), jnp.stack(gla_p),
            jnp.stack(kv_s), jnp.stack(win_s), jnp.stack(gla_s))
```

```python
import functools

import jax
import jax.numpy as jnp
from jax import lax
from jax.experimental import pallas as pl
from jax.experimental.pallas import tpu as pltpu

f32 = jnp.float32
bf16 = jnp.bfloat16
i32 = jnp.int32

GLA_HEADS = 4
GLA_DK = 128
GLA_DV = 256
GLA_RANK = 16
GLA_TAU = 16.0
GLA_CHUNK = 64
GLA_SUB = 16
NSA_HEADS = 16
NSA_KV_HEADS = 4
NSA_DH = 64
CMP_BLOCK = 64
TOP_N = 16
WINDOW = 512
Q_BLOCK = 128
ROPE_THETA = 500000.0
ROPE_DIMS = NSA_DH // 4
N_EXPERTS = 8
TOP_K = 2
EPS = 1e-6
NEG_INF = -1e30
FORCED_SCORE = 1e4

LANES = 128
TS_PAD = 16
VMEM_LIMIT = 56 * 1024 * 1024

C_GQ, C_GK, C_GV, C_GR, C_NQ, C_KVC, C_KVW, C_NG, C_GA, C_MG = 0, 512, 1024, 2048, 3072, 4096, 5120, 5632, 5680, 6144
N_PROJ = 10240
PAGES_PER_STEP = 8


def _cp(sem):
    return pltpu.CompilerParams(dimension_semantics=sem, vmem_limit_bytes=VMEM_LIMIT)


def _dot(a, b):
    return jnp.dot(a, b, preferred_element_type=f32)


def _dot_nt(a, b):
    return lax.dot_general(a, b, (((1,), (1,)), ((), ())), preferred_element_type=f32)


def _dot_tn(a, b):
    return lax.dot_general(a, b, (((0,), (0,)), ((), ())), preferred_element_type=f32)


def _split2(x):
    hi = x.astype(bf16)
    lo = (x - hi.astype(f32)).astype(bf16)
    return hi, lo


def _split3(x):
    hi = x.astype(bf16)
    r = x - hi.astype(f32)
    mid = r.astype(bf16)
    lo = (r - mid.astype(f32)).astype(bf16)
    return hi, mid, lo


def _dotp(fn, a, b, hp):
    if not hp:
        return fn(a.astype(bf16), b.astype(bf16))
    ah, al = _split2(a)
    bh, bl = _split2(b)
    return fn(ah, bh) + fn(al, bh) + fn(ah, bl)


def _mm(a, ws):
    if len(ws) == 1:
        return _dot(a.astype(bf16), ws[0])
    ah, al = _split2(a)
    return _dot(ah, ws[0]) + _dot(al, ws[0]) + _dot(ah, ws[1])


def _silu(x):
    return x * jax.nn.sigmoid(x)


def _group_ones(n, group):
    r = lax.broadcasted_iota(i32, (n, n), 0) // group
    c = lax.broadcasted_iota(i32, (n, n), 1) // group
    return jnp.where(r == c, 1.0, 0.0).astype(bf16)


def _rms64(x, gain):
    e = _group_ones(LANES, NSA_DH)
    hi, lo = _split2(x * x)
    ssq = _dot(hi, e) + _dot(lo, e)
    return x * lax.rsqrt(ssq * (1.0 / NSA_DH) + EPS) * gain


def _modulated(x_ref, g_ref, sh_ref, sc_ref):
    x = x_ref[0]
    y = x * lax.rsqrt(jnp.mean(x * x, axis=-1, keepdims=True) + EPS) * g_ref[...]
    return y * (1.0 + sc_ref[0]) + sh_ref[0]


def _mod_specs(t, tm, d, per_row):
    if per_row:
        return pl.BlockSpec((1, tm, d), lambda g, i, j: (g, i, 0))
    return pl.BlockSpec((1, 1, d), lambda g, i, j: (g, 0, 0))


def _ada_kernel(c_ref, w_ref, b_ref, o_ref):
    o_ref[0] = _dotp(_dot, c_ref[...], w_ref[0], True) + b_ref[0]


def ada_all(c_pad, w_ada, b_ada):
    nl, d, n = w_ada.shape
    tn = 1024
    return pl.pallas_call(
        _ada_kernel,
        out_shape=jax.ShapeDtypeStruct((nl, c_pad.shape[0], n), f32),
        grid=(nl, n // tn),
        in_specs=[pl.BlockSpec(c_pad.shape, lambda l, j: (0, 0)),
                  pl.BlockSpec((1, d, tn), lambda l, j: (l, 0, j)),
                  pl.BlockSpec((1, 1, tn), lambda l, j: (l, 0, j))],
        out_specs=pl.BlockSpec((1, c_pad.shape[0], tn), lambda l, j: (l, 0, j)),
        compiler_params=_cp(("arbitrary", "arbitrary")),
        name="ada",
    )(c_pad, w_ada, b_ada.reshape(nl, 1, n))


def _store_split(h, h_refs):
    if len(h_refs) == 1:
        h_refs[0][...] = h.astype(bf16)
    else:
        hi, lo = _split2(h)
        h_refs[0][...] = hi
        h_refs[1][...] = lo


def _mm_split(h_refs, w_refs):
    acc = _dot(h_refs[0][...], w_refs[0][...])
    if len(w_refs) == 2:
        acc = acc + _dot(h_refs[1][...], w_refs[0][...]) + _dot(h_refs[0][...], w_refs[1][...])
    return acc


def _proj_kernel(*refs, sig_from, nw):
    x_ref, g_ref, sh_ref, sc_ref = refs[:4]
    w_refs = refs[4:4 + nw]
    o_ref = refs[4 + nw]
    h_refs = refs[5 + nw:]
    j = pl.program_id(2)

    @pl.when(j == 0)
    def _():
        _store_split(_modulated(x_ref, g_ref, sh_ref, sc_ref), h_refs)

    acc = _mm_split(h_refs, w_refs)

    @pl.when(j < sig_from)
    def _():
        o_ref[0] = acc

    @pl.when(j >= sig_from)
    def _():
        o_ref[0] = jax.nn.sigmoid(acc)


def proj(x, g, shift, scale, ws):
    grp, t, d = x.shape
    n = ws[0].shape[1]
    tm, tn = min(t, 1024), 1024
    per_row = shift.shape[1] != 1
    nw = len(ws)
    return pl.pallas_call(
        functools.partial(_proj_kernel, sig_from=C_MG // tn, nw=nw),
        out_shape=jax.ShapeDtypeStruct((grp, t, n), f32),
        grid=(grp, t // tm, n // tn),
        in_specs=[pl.BlockSpec((1, tm, d), lambda g_, i, j: (g_, i, 0)),
                  pl.BlockSpec((1, d), lambda g_, i, j: (0, 0)),
                  _mod_specs(t, tm, d, per_row), _mod_specs(t, tm, d, per_row)]
        + [pl.BlockSpec((d, tn), lambda g_, i, j: (0, j))] * nw,
        out_specs=pl.BlockSpec((1, tm, tn), lambda g_, i, j: (g_, i, j)),
        scratch_shapes=[pltpu.VMEM((tm, d), bf16)] * nw,
        compiler_params=_cp(("arbitrary", "arbitrary", "arbitrary")),
        name="proj",
    )(x, g, shift, scale, *ws)


def _rope(y, cos, sa, sb):
    return y * cos + pltpu.roll(y, LANES - ROPE_DIMS // 2, 1) * sa + pltpu.roll(y, ROPE_DIMS // 2, 1) * sb


def _nsa_prep_kernel(nq_ref, kvc_ref, kvw_ref, cos_ref, sa_ref, sb_ref, gq_ref, gk_ref, qz_ref, oc_ref, ow_ref):
    cos, sa, sb = cos_ref[...], sa_ref[...], sb_ref[...]
    lane = lax.broadcasted_iota(i32, cos.shape, 1)
    left = lane < NSA_DH
    for m in range(NSA_HEADS // 2):
        y = _rope(_rms64(nq_ref[0, :, m * LANES:(m + 1) * LANES], gq_ref[...]), cos, sa, sb) * (NSA_DH ** -0.5)
        yr = pltpu.roll(y, NSA_DH, 1)
        if (m // 2) % 2 == 0:
            a, b = jnp.where(left, y, 0.0), jnp.where(left, yr, 0.0)
        else:
            a, b = jnp.where(left, 0.0, yr), jnp.where(left, 0.0, y)
        qz_ref[0, :, (2 * m) * LANES:(2 * m + 1) * LANES] = a.astype(qz_ref.dtype)
        qz_ref[0, :, (2 * m + 1) * LANES:(2 * m + 2) * LANES] = b.astype(qz_ref.dtype)
    for c in range(8):
        x = kvc_ref[0, :, c * LANES:(c + 1) * LANES]
        if (c // 2) % 2 == 0:
            x = _rope(_rms64(x, gk_ref[c // 4:c // 4 + 1, :]), cos, sa, sb)
        oc_ref[0, :, c * LANES:(c + 1) * LANES] = x
    for c in range(4):
        x = kvw_ref[0, :, c * LANES:(c + 1) * LANES]
        if c < 2:
            x = _rope(_rms64(x, gk_ref[2:3, :]), cos, sa, sb)
        ow_ref[0, :, c * LANES:(c + 1) * LANES] = x


def nsa_prep(z, cos, sa, sb, gq, gk, q_dtype):
    grp, t, _ = z.shape
    tm = min(t, 256)
    tab = pl.BlockSpec((tm, LANES), lambda g_, i: (i, 0))
    return pl.pallas_call(
        _nsa_prep_kernel,
        out_shape=(jax.ShapeDtypeStruct((grp, t, 2 * NSA_HEADS * NSA_DH), q_dtype),
                   jax.ShapeDtypeStruct((grp, t, 1024), f32),
                   jax.ShapeDtypeStruct((grp, t, 512), f32)),
        grid=(grp, t // tm),
        in_specs=[pl.BlockSpec((1, tm, 1024), lambda g_, i: (g_, i, C_NQ // 1024)),
                  pl.BlockSpec((1, tm, 1024), lambda g_, i: (g_, i, C_KVC // 1024)),
                  pl.BlockSpec((1, tm, 512), lambda g_, i: (g_, i, C_KVW // 512)),
                  tab, tab, tab,
                  pl.BlockSpec((1, LANES), lambda g_, i: (0, 0)),
                  pl.BlockSpec((3, LANES), lambda g_, i: (0, 0))],
        out_specs=(pl.BlockSpec((1, tm, 2 * NSA_HEADS * NSA_DH), lambda g_, i: (g_, i, 0)),
                   pl.BlockSpec((1, tm, 1024), lambda g_, i: (g_, i, 0)),
                   pl.BlockSpec((1, tm, 512), lambda g_, i: (g_, i, 0))),
        compiler_params=_cp(("arbitrary", "arbitrary")),
        name="nsa_prep",
    )(z, z, z, cos, sa, sb, gq, gk)


def _gla_kernel(*refs, chunk, sub, n_valid, has_init, hp):
    if has_init:
        q_ref, k_ref, v_ref, ga_ref, wa_ref, ba_ref, s0_ref, o_ref, so_ref, st_ref = refs
    else:
        q_ref, k_ref, v_ref, ga_ref, wa_ref, ba_ref, o_ref, so_ref, st_ref = refs
    i = pl.program_id(1)
    tb = q_ref.shape[1]
    ns = chunk // sub

    @pl.when(i == 0)
    def _():
        for h in range(GLA_HEADS):
            if has_init:
                st_ref[h] = s0_ref[0, h].T
            else:
                st_ref[h] = jnp.zeros((GLA_DV, GLA_DK), f32)

    ri = lax.broadcasted_iota(i32, (chunk, chunk), 0)
    ci = lax.broadcasted_iota(i32, (chunk, chunk), 1)
    ci_sub = lax.broadcasted_iota(i32, (sub, chunk), 1)
    tri =jnp.where(ri >= ci, 1.0, 0.0).astype(bf16)
    rrow = lax.broadcasted_iota(i32, (chunk, GLA_DK), 0)
    rmod = rrow % sub
    ga_off = C_GA % LANES

    def do_chunk(c, carry):
        r0 = pl.multiple_of(c * chunk, chunk)
        ga = ga_ref[0, pl.ds(r0, chunk), :][:, ga_off:ga_off + GLA_RANK]
        for h in range(GLA_HEADS):
            hs = slice(h * GLA_DK, (h + 1) * GLA_DK)
            q = q_ref[0, pl.ds(r0, chunk), hs] * (GLA_DK ** -0.5)
            k = k_ref[0, pl.ds(r0, chunk), hs]
            v = v_ref[0, pl.ds(r0, chunk), h * GLA_DV:(h + 1) * GLA_DV]
            x = _dotp(_dot, ga, wa_ref[:, hs], hp) + ba_ref[:, hs]
            la = (jnp.minimum(x, 0.0) - jnp.log1p(jnp.exp(-jnp.abs(x)))) * (1.0 / GLA_TAU)
            if n_valid < chunk:
                la = jnp.where(rrow < n_valid, la, 0.0)
                k = jnp.where(rrow < n_valid, k, 0.0)
            l1, l2, l3 = _split3(la)
            cum = _dot(tri, l1) + _dot(tri, l2) + _dot(tri, l3)
            parts = [jnp.zeros((sub, chunk), f32)]
            for sb_i in range(1, ns):
                lo = sb_i * sub
                m_i = cum[lo:lo + 1, :]
                q_i = q[lo:lo + sub] * jnp.exp(cum[lo:lo + sub] - m_i)
                k_i = k * jnp.exp(jnp.minimum(m_i - cum, 0.0))
                parts.append(jnp.where(ci_sub < lo, _dotp(_dot_nt, q_i, k_i, hp), 0.0))
            scores = parts[0] if ns == 1 else jnp.concatenate(parts, axis=0)
            for j in range(min(sub, n_valid)):
                kj = [jnp.broadcast_to(k[s * sub + j:s * sub + j + 1, :], (sub, GLA_DK)) for s in range(ns)]
                cj = [jnp.broadcast_to(cum[s * sub + j:s * sub + j + 1, :], (sub, GLA_DK)) for s in range(ns)]
                kj = kj[0] if ns == 1 else jnp.concatenate(kj, axis=0)
                cj = cj[0] if ns == 1 else jnp.concatenate(cj, axis=0)
                e = jnp.exp(jnp.where(rmod >= j, cum - cj, NEG_INF))
                val = jnp.sum(q * kj * e, axis=1, keepdims=True)
                scores = jnp.where(ci == (ri // sub) * sub + j, val, scores)
            st = st_ref[h]
            o = _dotp(_dot, scores, v, hp) + _dotp(_dot_nt, q * jnp.exp(cum), st, hp)
            o_ref[0, pl.ds(r0, chunk), h * GLA_DV:(h + 1) * GLA_DV] = o
            last = cum[chunk - 1:chunk, :]
            st_ref[h] = st * jnp.exp(last) + _dotp(_dot_tn, v, k * jnp.exp(last - cum), hp)
        return carry

    lax.fori_loop(0, tb // chunk, do_chunk, 0)

    @pl.when(i == pl.num_programs(1) - 1)
    def _():
        for h in range(GLA_HEADS):
            so_ref[0, h] = st_ref[h].T


def gla(z, wa, ba, s0, chunk, sub, n_valid, hp):
    b, t, _ = z.shape
    tb = min(t, 256)
    has_init = s0 is not None
    hk = GLA_HEADS * GLA_DK
    in_specs = [pl.BlockSpec((1, tb, hk), lambda b_, i: (b_, i, C_GQ // hk)),
                pl.BlockSpec((1, tb, hk), lambda b_, i: (b_, i, C_GK // hk)),
                pl.BlockSpec((1, tb, GLA_HEADS * GLA_DV), lambda b_, i: (b_, i, C_GV // (GLA_HEADS * GLA_DV))),
                pl.BlockSpec((1, tb, LANES), lambda b_, i: (b_, i, C_GA // LANES)),
                pl.BlockSpec((GLA_RANK, hk), lambda b_, i: (0, 0)),
                pl.BlockSpec((1, hk), lambda b_, i: (0, 0))]
    args = [z, z, z, z, wa, ba]
    if has_init:
        in_specs.append(pl.BlockSpec((1, GLA_HEADS, GLA_DK, GLA_DV), lambda b_, i: (b_, 0, 0, 0)))
        args.append(s0)
    return pl.pallas_call(
        functools.partial(_gla_kernel, chunk=chunk, sub=sub, n_valid=n_valid, has_init=has_init, hp=hp),
        out_shape=(jax.ShapeDtypeStruct((b, t, GLA_HEADS * GLA_DV), f32),
                   jax.ShapeDtypeStruct((b, GLA_HEADS, GLA_DK, GLA_DV), f32)),
        grid=(b, t // tb),
        in_specs=in_specs,
        out_specs=(pl.BlockSpec((1, tb, GLA_HEADS * GLA_DV), lambda b_, i: (b_, i, 0)),
                   pl.BlockSpec((1, GLA_HEADS, GLA_DK, GLA_DV), lambda b_, i: (b_, 0, 0, 0))),
        scratch_shapes=[pltpu.VMEM((GLA_HEADS, GLA_DV, GLA_DK), f32)],
        compiler_params=_cp(("arbitrary", "arbitrary")),
        name="gla",
    )(*args)


def _compress_core(x_refs, pe_ref, w1_refs, w2_refs, gk_ref, kc_ref, vc_ref, nblk):
    for c in range(4):
        br = c // 2

        def body(s, acc, c=c, br=br):
            rows = x_refs[c][pl.ds(s, nblk, stride=CMP_BLOCK), :] + pe_ref[br, pl.ds(s, 1), :]
            return acc + _mm(rows, [w[br, s] for w in w1_refs])

        acc = lax.fori_loop(0, CMP_BLOCK, body, jnp.zeros((nblk, 2 * LANES), f32), unroll=8)
        y = _mm(_silu(acc), [w[br] for w in w2_refs])
        if br == 0:
            kc_ref[0, :, (c % 2) * LANES:(c % 2 + 1) * LANES] = _rms64(y, gk_ref[...])
        else:
            vc_ref[0, :, (c % 2) * LANES:(c % 2 + 1) * LANES] = y


def _compress_prompt_kernel(x0, x1, x2, x3, pe_ref, w1_ref, w2_ref, gk_ref, kc_ref, vc_ref):
    nblk = x0.shape[1] // CMP_BLOCK
    _compress_core([x0.at[0], x1.at[0], x2.at[0], x3.at[0]], pe_ref, (w1_ref,), (w2_ref,), gk_ref, kc_ref, vc_ref, nblk)


def compress_prompt(kvc, pe2, w1c, w2c, gkc):
    b, t, _ = kvc.shape
    nblk = t // CMP_BLOCK
    cst = lambda shape: pl.BlockSpec(shape, lambda b_: (0,) * len(shape))
    xs = [pl.BlockSpec((1, t, LANES), functools.partial(lambda b_, c: (b_, 0, c), c=c)) for c in range(4)]
    return pl.pallas_call(
        _compress_prompt_kernel,
        out_shape=(jax.ShapeDtypeStruct((b, nblk, 256), f32), jax.ShapeDtypeStruct((b, nblk, 256), f32)),
        grid=(b,),
        in_specs=xs + [cst(pe2.shape), cst(w1c.shape), cst(w2c.shape), cst(gkc.shape)],
        out_specs=(pl.BlockSpec((1, nblk, 256), lambda b_: (b_, 0, 0)), pl.BlockSpec((1, nblk, 256), lambda b_: (b_, 0, 0))),
        compiler_params=_cp(("arbitrary",)),
        name="compress_prompt",
    )(kvc, kvc, kvc, kvc, pe2, w1c, w2c, gkc)


def _compress_sample_kernel(pt_ref, *refs, pages_macro):
    pages = refs[:PAGES_PER_STEP]
    pe_ref, w1h_ref, w1l_ref, w2h_ref, w2l_ref, gk_ref, kc_ref, vc_ref, x0, x1, x2, x3 = refs[PAGES_PER_STEP:]
    st = pl.program_id(2)
    xs = [x0, x1, x2, x3]
    psz = pages[0].shape[2]
    for kk in range(PAGES_PER_STEP):
        r0 = pl.multiple_of((st * PAGES_PER_STEP + kk) * psz, psz)
        for c in range(4):
            xs[c][pl.ds(r0, psz), :] = pages[kk][0, 0, :, c * LANES:(c + 1) * LANES]

    @pl.when(st == pl.num_programs(2) - 1)
    def _():
        _compress_core(xs, pe_ref, (w1h_ref, w1l_ref), (w2h_ref, w2l_ref), gk_ref, kc_ref, vc_ref,
                       pages_macro * psz // CMP_BLOCK)


def compress_sample(page_flat, cache4, layer, nb, n_pages, pe2, w1cs, w2cs, gkc):
    psz = cache4.shape[2]
    pages_macro = min(n_pages, 128)
    n_macro = n_pages // pages_macro
    steps = pages_macro // PAGES_PER_STEP
    nblk_macro = pages_macro * psz // CMP_BLOCK
    nblk = n_pages * psz // CMP_BLOCK

    def page_map(b_, m, s, pt, kk):
        return (layer, pt[b_ * n_pages + m * pages_macro + s * PAGES_PER_STEP + kk], 0, 0)

    cst = lambda shape: pl.BlockSpec(shape, lambda b_, m, s, pt: (0,) * len(shape), pipeline_mode=pl.Buffered(1))
    pspecs = [pl.BlockSpec((1, 1, psz, 512), functools.partial(page_map, kk=kk)) for kk in range(PAGES_PER_STEP)]
    ospec = pl.BlockSpec((1, nblk_macro, 256), lambda b_, m, s, pt: (b_, m, 0))
    return pl.pallas_call(
        functools.partial(_compress_sample_kernel, pages_macro=pages_macro),
        out_shape=(jax.ShapeDtypeStruct((nb, nblk, 256), f32), jax.ShapeDtypeStruct((nb, nblk, 256), f32)),
        grid_spec=pltpu.PrefetchScalarGridSpec(
            num_scalar_prefetch=1, grid=(nb, n_macro, steps),
            in_specs=pspecs + [cst(pe2.shape), cst(w1cs[0].shape), cst(w1cs[1].shape),
                               cst(w2cs[0].shape), cst(w2cs[1].shape), cst(gkc.shape)],
            out_specs=(ospec, ospec),
            scratch_shapes=[pltpu.VMEM((pages_macro * psz, LANES), f32) for _ in range(4)]),
        compiler_params=_cp(("arbitrary", "arbitrary", "arbitrary")),
        name="compress_sample",
    )(page_flat, *([cache4] * PAGES_PER_STEP), pe2, *w1cs, *w2cs, gkc)


def _stack_heads(qz, rows):
    return jnp.concatenate([qz[:, h * LANES:(h + 1) * LANES] for h in range(8)], axis=0)


def _unstack_heads(o, rows):
    lane = lax.broadcasted_iota(i32, (rows, LANES), 1)
    left = lane < NSA_DH
    outs = []
    for p in range(4):
        a = o[(2 * p) * rows:(2 * p + 1) * rows]
        b = o[(2 * p + 1) * rows:(2 * p + 2) * rows]
        if p < 2:
            outs.append(jnp.where(left, a, pltpu.roll(b, NSA_DH, 1)))
        else:
            outs.append(jnp.where(left, pltpu.roll(a, NSA_DH, 1), b))
    return jnp.concatenate(outs, axis=1)


def _rank_loop(sc_ref, nblk):
    shape = sc_ref.shape
    blk = lax.broadcasted_iota(i32, shape, 0)
    sc = sc_ref[...]

    def body(j, rank):
        rj = jnp.broadcast_to(sc_ref[pl.ds(j, 1), :], shape)
        beats = (rj > sc) | ((rj == sc) & (blk > j))
        return rank + jnp.where(beats, 1, 0)

    return lax.fori_loop(0, nblk, body, jnp.zeros(shape, i32), unroll=8)


def _cmpsel_prompt_kernel(qz_ref, kc_ref, vc_ref, oc_ref, sel_ref, sc_ref):
    qb = pl.program_id(1)
    tq = qz_ref.shape[1]
    nf = kc_ref.shape[1]
    tpos = qb * tq + lax.broadcasted_iota(i32, (8 * tq, nf), 0) % tq
    blk_end = (lax.broadcasted_iota(i32, (8 * tq, nf), 1) + 1) * CMP_BLOCK - 1
    vis = blk_end <= tpos
    for kp in range(2):
        q = _stack_heads(qz_ref[0, :, kp * 8 * LANES:(kp + 1) * 8 * LANES], tq)
        kc = kc_ref[0, :, kp * LANES:(kp + 1) * LANES].astype(bf16)
        vc = vc_ref[0, :, kp * LANES:(kp + 1) * LANES].astype(bf16)
        s = jnp.where(vis, _dot_nt(q, kc), NEG_INF)
        e = jnp.where(vis, jnp.exp(s - jnp.max(s, axis=1, keepdims=True)), 0.0)
        p = e / jnp.maximum(jnp.sum(e, axis=1, keepdims=True), 1e-30)
        oc_ref[0, :, kp * 4 * LANES:(kp + 1) * 4 * LANES] = _unstack_heads(_dot(p.astype(bf16), vc), tq)
        imp_a = p[0:tq] + p[tq:2 * tq] + p[2 * tq:3 * tq] + p[3 * tq:4 * tq]
        imp_b = p[4 * tq:5 * tq] + p[5 * tq:6 * tq] + p[6 * tq:7 * tq] + p[7 * tq:8 * tq]
        imp_t = jnp.concatenate([imp_a, imp_b], axis=1).T
        blk = lax.broadcasted_iota(i32, (nf, tq), 0)
        cur = (qb * tq + lax.broadcasted_iota(i32, (nf, tq), 1)) // CMP_BLOCK
        forced = (blk == 0) | (blk == cur) | (blk == cur - 1)
        sels = []
        for half in range(2):
            sc = jnp.where(blk <= cur, jnp.where(forced, FORCED_SCORE, imp_t[half * nf:(half + 1) * nf]), NEG_INF)
            sc_ref[...] = sc
            rank = _rank_loop(sc_ref, nf)
            sels.append(jnp.where((rank < TOP_N) & (sc > 0.5 * NEG_INF), 1.0, 0.0))
        sel_ref[0, :, kp * 2 * nf:(kp + 1) * 2 * nf] = jnp.concatenate(sels, axis=0).T.astype(bf16)


def cmpsel_prompt(qz, kc, vc):
    b, t, _ = qz.shape
    nf = kc.shape[1]
    tq = Q_BLOCK
    return pl.pallas_call(
        _cmpsel_prompt_kernel,
        out_shape=(jax.ShapeDtypeStruct((b, t, NSA_HEADS * NSA_DH), f32),
                   jax.ShapeDtypeStruct((b, t, NSA_KV_HEADS * nf), bf16)),
        grid=(b, t // tq),
        in_specs=[pl.BlockSpec((1, tq, 2 * NSA_HEADS * NSA_DH), lambda b_, i: (b_, i, 0)),
                  pl.BlockSpec((1, nf, 256), lambda b_, i: (b_, 0, 0)),
                  pl.BlockSpec((1, nf, 256), lambda b_, i: (b_, 0, 0))],
        out_specs=(pl.BlockSpec((1, tq, NSA_HEADS * NSA_DH), lambda b_, i: (b_, i, 0)),
                   pl.BlockSpec((1, tq, NSA_KV_HEADS * nf), lambda b_, i: (b_, i, 0))),
        scratch_shapes=[pltpu.VMEM((nf, tq), f32)],
        compiler_params=_cp(("arbitrary", "arbitrary")),
        name="cmpsel_prompt",
    )(qz, kc, vc)


def _attn_prompt_kernel(*refs, mode, kt):
    if mode == "slc":
        qz_ref, k_ref, v_ref, sel_ref, o_ref, m_ref, l_ref, acc_ref = refs
    else:
        qz_ref, k_ref, v_ref, o_ref, m_ref, l_ref, acc_ref = refs
    qb = pl.program_id(2)
    tq = qz_ref.shape[1]
    q = _stack_heads(qz_ref[0], tq)
    tpos = qb * tq + lax.broadcasted_iota(i32, (tq, kt), 0)
    kiota = lax.broadcasted_iota(i32, (tq, kt), 1)
    m_ref[...] = jnp.full(m_ref.shape, NEG_INF, f32)
    l_ref[...] = jnp.zeros(l_ref.shape, f32)
    acc_ref[...] = jnp.zeros(acc_ref.shape, f32)

    def tile(k0):
        k = k_ref[0, pl.ds(k0, kt), :].astype(bf16)
        v = v_ref[0, pl.ds(k0, kt), :].astype(bf16)
        kpos = k0 + kiota
        if mode == "slc":
            nf = sel_ref.shape[2] // 2
            er = lax.broadcasted_iota(i32, (2 * nf, kt), 0)
            ec = (k0 + lax.broadcasted_iota(i32, (2 * nf, kt), 1)) // CMP_BLOCK
            sel = sel_ref[0]
            ma = _dot(sel, jnp.where(er == ec, 1.0, 0.0).astype(bf16))
            mb = _dot(sel, jnp.where(er - nf == ec, 1.0, 0.0).astype(bf16))
            causal = kpos <= tpos
            biases = (jnp.where(causal & (ma > 0.5), 0.0, NEG_INF), jnp.where(causal & (mb > 0.5), 0.0, NEG_INF))
        else:
            ba = jnp.where((kpos <= tpos) & (kpos >= tpos - WINDOW), 0.0, NEG_INF)
            biases = (ba, ba)
        s_all = _dot_nt(q, k)
        heads = [s_all[h * tq:(h + 1) * tq] + biases[h // 4] for h in range(8)]
        m_cur = jnp.concatenate([jnp.broadcast_to(jnp.max(s, axis=1, keepdims=True), (tq, LANES)) for s in heads], axis=0)
        m_old = m_ref[...]
        m_new = jnp.maximum(m_old, m_cur)
        alpha = jnp.exp(m_old - m_new)
        m_ref[...] = m_new
        ps, sums = [], []
        for h in range(8):
            m_h = m_new[h * tq:(h + 1) * tq]
            p = jnp.exp(heads[h] - jnp.concatenate([m_h] * (kt // LANES), axis=1))
            sums.append(jnp.broadcast_to(jnp.sum(p, axis=1, keepdims=True), (tq, LANES)))
            ps.append(p.astype(bf16))
        l_ref[...] = alpha * l_ref[...] + jnp.concatenate(sums, axis=0)
        acc_ref[...] = alpha * acc_ref[...] + _dot(jnp.concatenate(ps, axis=0), v)

    if mode == "slc":
        def body(t_i, c):
            tile(pl.multiple_of(t_i * kt, kt))
            return c
        lax.fori_loop(0, (qb * tq + tq + kt - 1) // kt, body, 0)
    else:
        tile(pl.multiple_of(jnp.maximum(qb * tq - WINDOW, 0), tq))
    o_ref[0] = _unstack_heads(acc_ref[...] / l_ref[...], tq)


def attn_prompt(qz, kv, sel, mode):
    b, t, _ = qz.shape
    tq = Q_BLOCK
    if mode == "slc":
        kt, kcol, vcol = 512, 4, 6
    else:
        kt, kcol, vcol = WINDOW + tq, 0, 2
        assert t >= kt
    in_specs = [pl.BlockSpec((1, tq, 8 * LANES), lambda b_, kp, i: (b_, i, kp)),
                pl.BlockSpec((1, t, LANES), lambda b_, kp, i: (b_, 0, kcol + kp)),
                pl.BlockSpec((1, t, LANES), lambda b_, kp, i: (b_, 0, vcol + kp))]
    args = [qz, kv, kv]
    if mode == "slc":
        nf2 = sel.shape[2] // 2
        in_specs.append(pl.BlockSpec((1, tq, nf2), lambda b_, kp, i: (b_, i, kp)))
        args.append(sel)
    return pl.pallas_call(
        functools.partial(_attn_prompt_kernel, mode=mode, kt=kt),
        out_shape=jax.ShapeDtypeStruct((b, t, NSA_HEADS * NSA_DH), f32),
        grid=(b, 2, t // tq),
        in_specs=in_specs,
        out_specs=pl.BlockSpec((1, tq, 4 * LANES), lambda b_, kp, i: (b_, i, kp)),
        scratch_shapes=[pltpu.VMEM((8 * tq, LANES), f32)] * 3,
        compiler_params=_cp(("arbitrary", "arbitrary", "arbitrary")),
        name="attn_" + mode,
    )(*args)


def _cmpsel_sample_kernel(qz_ref, kc_ref, vc_ref, oc_ref, sel_ref, sc_ref, *, q_pos0):
    ts = qz_ref.shape[1]
    nf = kc_ref.shape[1]
    nfp = sc_ref.shape[0]
    rows = 8 * ts
    tpos = q_pos0 + lax.broadcasted_iota(i32, (rows, nf), 0) % ts
    vis = (lax.broadcasted_iota(i32, (rows, nf), 1) + 1) * CMP_BLOCK - 1 <= tpos
    for kp in range(2):
        q = _stack_heads(qz_ref[0, :, kp * 8 * LANES:(kp + 1) * 8 * LANES], ts)
        kc = kc_ref[0, :, kp * LANES:(kp + 1) * LANES]
        vc = vc_ref[0, :, kp * LANES:(kp + 1) * LANES]
        s = jnp.where(vis, _dotp(_dot_nt, q, kc, True), NEG_INF)
        e = jnp.where(vis, jnp.exp(s - jnp.max(s, axis=1, keepdims=True)), 0.0)
        p = e / jnp.maximum(jnp.sum(e, axis=1, keepdims=True), 1e-30)
        oc_ref[0, :, kp * 4 * LANES:(kp + 1) * 4 * LANES] = _unstack_heads(_dotp(_dot, p, vc, True), ts)
        imp_a = p[0:ts] + p[ts:2 * ts] + p[2 * ts:3 * ts] + p[3 * ts:4 * ts]
        imp_b = p[4 * ts:5 * ts] + p[5 * ts:6 * ts] + p[6 * ts:7 * ts] + p[7 * ts:8 * ts]
        imp = jnp.concatenate([imp_a] * 4 + [imp_b] * 4, axis=0)
        if nfp > nf:
            imp = jnp.concatenate([imp, jnp.zeros((rows, nfp - nf), f32)], axis=1)
        imp_t = jnp.concatenate([imp[:, c * LANES:(c + 1) * LANES].T for c in range(nfp // LANES)], axis=0)
        blk = lax.broadcasted_iota(i32, (nfp, rows), 0)
        cur = (q_pos0 + lax.broadcasted_iota(i32, (nfp, rows), 1) % ts) // CMP_BLOCK
        forced = (blk == 0) | (blk == cur) | (blk == cur - 1)
        sc = jnp.where(blk < nf, jnp.where(forced, FORCED_SCORE, imp_t), NEG_INF)
        sc_ref[...] = sc
        rank = _rank_loop(sc_ref, nf)
        sel_ref[0, kp] = jnp.where((rank < TOP_N - 1) & (sc > 0.5 * NEG_INF), 1.0, 0.0)


def cmpsel_sample(qz3, kc, vc, q_pos0):
    nb, ts, _ = qz3.shape
    nf = kc.shape[1]
    nfp = -(-nf // LANES) * LANES
    return pl.pallas_call(
        functools.partial(_cmpsel_sample_kernel, q_pos0=q_pos0),
        out_shape=(jax.ShapeDtypeStruct((nb, ts, NSA_HEADS * NSA_DH), f32),
                   jax.ShapeDtypeStruct((nb, 2, nfp, 8 * ts), f32)),
        grid=(nb,),
        in_specs=[pl.BlockSpec((1, ts, 2 * NSA_HEADS * NSA_DH), lambda b_: (b_, 0, 0)),
                  pl.BlockSpec((1, nf, 256), lambda b_: (b_, 0, 0)),
                  pl.BlockSpec((1, nf, 256), lambda b_: (b_, 0, 0))],
        out_specs=(pl.BlockSpec((1, ts, NSA_HEADS * NSA_DH), lambda b_: (b_, 0, 0)),
                   pl.BlockSpec((1, 2, nfp, 8 * ts), lambda b_: (b_, 0, 0, 0))),
        scratch_shapes=[pltpu.VMEM((nfp, 8 * ts), f32)],
        compiler_params=_cp(("arbitrary",)),
        name="cmpsel_sample",
    )(qz3, kc, vc)


def _slc_sample_kernel(pt_ref, *refs, n_new):
    npg = PAGES_PER_STEP
    kpages, vpages = refs[:npg], refs[npg:2 * npg]
    qz_ref, sel_ref, kn_ref, vn_ref, o_ref, m_ref, l_ref, acc_ref = refs[2 * npg:]
    st = pl.program_id(2)
    ts = qz_ref.shape[1]
    cols = 8 * ts
    q = _stack_heads(qz_ref[0], ts)

    @pl.when(st == 0)
    def _():
        m_ref[...] = jnp.full(m_ref.shape, NEG_INF, f32)
        l_ref[...] = jnp.zeros(l_ref.shape, f32)
        acc_ref[...] = jnp.zeros(acc_ref.shape, f32)

    qh, ql = _split2(q)
    q2 = jnp.concatenate([qh, ql], axis=0)

    def update(k, v, bias):
        kh, kl = _split2(k)
        s2 = _dot_nt(kh, q2)
        s = s2[:, :cols] + s2[:, cols:] + _dot_nt(kl, qh) + bias
        m_old = m_ref[...]
        m_new = jnp.maximum(m_old, jnp.max(s, axis=0, keepdims=True))
        p = jnp.exp(s - m_new)
        alpha = jnp.exp(m_old - m_new)
        l_ref[...] = alpha * l_ref[...] + jnp.sum(p, axis=0, keepdims=True)
        vh, vl = _split2(v)
        ph, pl_ = _split2(p)
        pv2 = _dot_tn(vh, jnp.concatenate([ph, pl_], axis=1))
        acc_ref[...] = alpha * acc_ref[...] + pv2[:, :cols] + pv2[:, cols:] + _dot_tn(vl, ph)
        m_ref[...] = m_new

    k = jnp.concatenate([r[0, 0] for r in kpages], axis=0)
    v = jnp.concatenate([r[0, 0] for r in vpages], axis=0)
    psz = kpages[0].shape[2]
    bps = npg * psz // CMP_BLOCK
    sel = sel_ref[0, 0]
    er = lax.broadcasted_iota(i32, (bps * CMP_BLOCK, bps), 0) // CMP_BLOCK
    ec = lax.broadcasted_iota(i32, (bps * CMP_BLOCK, bps), 1)
    msk = _dot(jnp.where(er == ec, 1.0, 0.0).astype(bf16), sel.astype(bf16))
    bias = jnp.where(msk > 0.5, 0.0, NEG_INF)
    update(k, v, bias)

    @pl.when(st == pl.num_programs(2) - 1)
    def _():
        kn = kn_ref[0]
        vn = vn_ref[0]
        j = lax.broadcasted_iota(i32, (ts, cols), 0)
        tok = lax.broadcasted_iota(i32, (ts, cols), 1) % ts
        update(kn, vn, jnp.where((j <= tok) & (j < n_new), 0.0, NEG_INF))
        o = (acc_ref[...] / l_ref[...]).T
        o_ref[0] = _unstack_heads(o, ts)


def slc_sample(page_flat, cache4, layer, n_pages, qz3, sel_t, kvc3, n_new):
    nb, ts, _ = qz3.shape
    psz = cache4.shape[2]
    steps = n_pages // PAGES_PER_STEP
    bps = PAGES_PER_STEP * psz // CMP_BLOCK
    cols = 8 * ts

    def page_map(b_, kp, s, pt, kk, col):
        return (layer, pt[b_ * n_pages + s * PAGES_PER_STEP + kk], 0, col + kp)

    kspecs = [pl.BlockSpec((1, 1, psz, LANES), functools.partial(page_map, kk=kk, col=4)) for kk in range(PAGES_PER_STEP)]
    vspecs = [pl.BlockSpec((1, 1, psz, LANES), functools.partial(page_map, kk=kk, col=6)) for kk in range(PAGES_PER_STEP)]
    return pl.pallas_call(
        functools.partial(_slc_sample_kernel, n_new=n_new),
        out_shape=jax.ShapeDtypeStruct((nb, ts, NSA_HEADS * NSA_DH), f32),
        grid_spec=pltpu.PrefetchScalarGridSpec(
            num_scalar_prefetch=1, grid=(nb, 2, steps),
            in_specs=kspecs + vspecs + [
                pl.BlockSpec((1, ts, 8 * LANES), lambda b_, kp, s, pt: (b_, 0, kp)),
                pl.BlockSpec((1, 1, bps, cols), lambda b_, kp, s, pt: (b_, kp, s, 0)),
                pl.BlockSpec((1, ts, LANES), lambda b_, kp, s, pt: (b_, 0, 4 + kp)),
                pl.BlockSpec((1, ts, LANES), lambda b_, kp, s, pt: (b_, 0, 6 + kp))],
            out_specs=pl.BlockSpec((1, ts, 4 * LANES), lambda b_, kp, s, pt: (b_, 0, kp)),
            scratch_shapes=[pltpu.VMEM((1, cols), f32), pltpu.VMEM((1, cols), f32), pltpu.VMEM((LANES, cols), f32)]),
        compiler_params=_cp(("arbitrary", "arbitrary", "arbitrary")),
        name="slc_sample",
    )(page_flat, *([cache4] * (2 * PAGES_PER_STEP)), qz3, sel_t, kvc3, kvc3)


def _win_sample_kernel(qz_ref, old_ref, new_ref, o_ref, nw_ref, *, n_new):
    ts = qz_ref.shape[1]
    wb = old_ref.shape[2]
    rows = 8 * ts
    old = old_ref[0, 0]
    new = new_ref[0]
    tok_o = lax.broadcasted_iota(i32, (rows, wb), 0) % ts
    j_o = lax.broadcasted_iota(i32, (rows, wb), 1)
    bias_o = jnp.where((j_o >= tok_o + (wb - WINDOW)), 0.0, NEG_INF)
    tok_n = lax.broadcasted_iota(i32, (rows, ts), 0) % ts
    j_n = lax.broadcasted_iota(i32, (rows, ts), 1)
    bias_n = jnp.where((j_n <= tok_n) & (j_n < n_new), 0.0, NEG_INF)
    for kp in range(2):
        q = _stack_heads(qz_ref[0, :, kp * 8 * LANES:(kp + 1) * 8 * LANES], ts)
        ko = old[:, kp * LANES:(kp + 1) * LANES]
        vo = old[:, 256 + kp * LANES:256 + (kp + 1) * LANES]
        kn = new[:, kp * LANES:(kp + 1) * LANES]
        vn = new[:, 256 + kp * LANES:256 + (kp + 1) * LANES]
        so = _dotp(_dot_nt, q, ko, True) + bias_o
        sn = _dotp(_dot_nt, q, kn, True) + bias_n
        m = jnp.maximum(jnp.max(so, axis=1, keepdims=True), jnp.max(sn, axis=1, keepdims=True))
        po, pn = jnp.exp(so - m), jnp.exp(sn - m)
        den = jnp.sum(po, axis=1, keepdims=True) + jnp.sum(pn, axis=1, keepdims=True)
        o = (_dotp(_dot, po, vo, True) + _dotp(_dot, pn, vn, True)) / den
        o_ref[0, :, kp * 4 * LANES:(kp + 1) * 4 * LANES] = _unstack_heads(o, ts)
    both = jnp.concatenate([old, new], axis=0)
    nw_ref[0, 0] = pltpu.roll(both, wb + ts - n_new, 0)[0:wb]


def win_sample(qz3, state_win4, layer, kvw3, n_new):
    nb, ts, _ = qz3.shape
    wb = state_win4.shape[2]
    return pl.pallas_call(
        functools.partial(_win_sample_kernel, n_new=n_new),
        out_shape=(jax.ShapeDtypeStruct((nb, ts, NSA_HEADS * NSA_DH), f32),
                   jax.ShapeDtypeStruct((1, nb, wb, 512), f32)),
        grid=(nb,),
        in_specs=[pl.BlockSpec((1, ts, 2 * NSA_HEADS * NSA_DH), lambda b_: (b_, 0, 0)),
                  pl.BlockSpec((1, 1, wb, 512), lambda b_: (layer, b_, 0, 0)),
                  pl.BlockSpec((1, ts, 512), lambda b_: (b_, 0, 0))],
        out_specs=(pl.BlockSpec((1, ts, NSA_HEADS * NSA_DH), lambda b_: (b_, 0, 0)),
                   pl.BlockSpec((1, 1, wb, 512), lambda b_: (0, b_, 0, 0))),
        compiler_params=_cp(("arbitrary",)),
        name="win_sample",
    )(qz3, state_win4, kvw3)


def _mix1_kernel(*refs, nw):
    og_ref, r_ref, ng_ref, mg0_ref, mg1_ref, oc_ref, os_ref, ow_ref, gg_ref = refs[:9]
    wa_refs, wb_refs = refs[9:9 + nw], refs[9 + nw:9 + 2 * nw]
    o_ref = refs[9 + 2 * nw]
    a_refs, b_refs = refs[10 + 2 * nw:10 + 3 * nw], refs[10 + 3 * nw:]
    j = pl.program_id(2)

    @pl.when(j == 0)
    def _():
        ys = []
        for h in range(GLA_HEADS):
            hs = slice(h * GLA_DV, (h + 1) * GLA_DV)
            x = og_ref[0, :, hs]
            y = x * lax.rsqrt(jnp.mean(x * x, axis=-1, keepdims=True) + EPS) * gg_ref[:, hs]
            ys.append(y * _silu(r_ref[0, :, hs]))
        _store_split(jnp.concatenate(ys, axis=1), a_refs)
        ghi, glo = _split2(jax.nn.sigmoid(ng_ref[0]))
        nh = NSA_HEADS * NSA_DH
        er = lax.broadcasted_iota(i32, (LANES, nh), 0)
        ec = lax.broadcasted_iota(i32, (LANES, nh), 1) // NSA_DH
        acc = None
        for br, ref in enumerate((oc_ref, os_ref, ow_ref)):
            ex = jnp.where(er == br * NSA_HEADS + ec, 1.0, 0.0).astype(bf16)
            term = (_dot(ghi, ex) + _dot(glo, ex)) * ref[0]
            acc = term if acc is None else acc + term
        _store_split(acc, b_refs)

    o_ref[0] = (mg0_ref[0] * _mm_split(a_refs, wa_refs) + mg1_ref[0] * _mm_split(b_refs, wb_refs)).astype(o_ref.dtype)


def mix1(o_gla, z, o_cmp, o_slc, o_win, gg, was, wbs):
    grp, t, n1 = o_gla.shape
    d = was[0].shape[1]
    nw = len(was)
    tm, tn = min(t, 512), 1024
    row = lambda w, c: pl.BlockSpec((1, tm, w), functools.partial(lambda g_, i, j, c: (g_, i, c), c=c))
    return pl.pallas_call(
        functools.partial(_mix1_kernel, nw=nw),
        out_shape=jax.ShapeDtypeStruct((grp, t, d), bf16 if nw == 1 else f32),
        grid=(grp, t // tm, d // tn),
        in_specs=[row(n1, 0), row(1024, C_GR // 1024), row(LANES, C_NG // LANES),
                  pl.BlockSpec((1, tm, tn), lambda g_, i, j: (g_, i, C_MG // tn + j)),
                  pl.BlockSpec((1, tm, tn), lambda g_, i, j: (g_, i, (C_MG + d) // tn + j)),
                  row(n1, 0), row(n1, 0), row(n1, 0),
                  pl.BlockSpec((1, n1), lambda g_, i, j: (0, 0))]
        + [pl.BlockSpec((n1, tn), lambda g_, i, j: (0, j))] * (2 * nw),
        out_specs=pl.BlockSpec((1, tm, tn), lambda g_, i, j: (g_, i, j)),
        scratch_shapes=[pltpu.VMEM((tm, n1), bf16)] * (2 * nw),
        compiler_params=_cp(("arbitrary", "arbitrary", "arbitrary")),
        name="mix1",
    )(o_gla, z, z, z, z, o_cmp, o_slc, o_win, gg, *was, *wbs)


def _mm_res_kernel(*refs, nw):
    a_ref = refs[0]
    w_refs = refs[1:1 + nw]
    x_ref, gt_ref, o_ref = refs[1 + nw:]
    o_ref[0] = x_ref[0] + gt_ref[0] * _mm(a_ref[0], [w[...] for w in w_refs])


def mm_res(a, ws, x, gate):
    grp, t, k = a.shape
    d = ws[0].shape[1]
    nw = len(ws)
    tm, tn = min(t, 1024), 512
    per_row = gate.shape[1] != 1
    return pl.pallas_call(
        functools.partial(_mm_res_kernel, nw=nw),
        out_shape=jax.ShapeDtypeStruct((grp, t, d), f32),
        grid=(grp, t // tm, d // tn),
        in_specs=[pl.BlockSpec((1, tm, k), lambda g_, i, j: (g_, i, 0))]
        + [pl.BlockSpec((k, tn), lambda g_, i, j: (0, j))] * nw
        + [pl.BlockSpec((1, tm, tn), lambda g_, i, j: (g_, i, j)),
           pl.BlockSpec((1, tm, tn), lambda g_, i, j: (g_, i, j)) if per_row
           else pl.BlockSpec((1, 1, tn), lambda g_, i, j: (g_, 0, j))],
        out_specs=pl.BlockSpec((1, tm, tn), lambda g_, i, j: (g_, i, j)),
        compiler_params=_cp(("arbitrary", "arbitrary", "arbitrary")),
        name="mm_res",
    )(a, *ws, x, gate)


def _ffn1_kernel(*refs, nw):
    x_ref, g_ref, sh_ref, sc_ref = refs[:4]
    wg_refs, wu_refs = refs[4:4 + nw], refs[4 + nw:4 + 2 * nw]
    o_ref = refs[4 + 2 * nw]
    h_refs = refs[5 + 2 * nw:]

    @pl.when(pl.program_id(2) == 0)
    def _():
        _store_split(_modulated(x_ref, g_ref, sh_ref, sc_ref), h_refs)

    o_ref[0] = (_silu(_mm_split(h_refs, wg_refs)) * _mm_split(h_refs, wu_refs)).astype(o_ref.dtype)


def ffn1(x, g, shift, scale, wgs, wus):
    grp, t, d = x.shape
    f = wgs[0].shape[1]
    nw = len(wgs)
    tm, tn = min(t, 1024), 512
    per_row = shift.shape[1] != 1
    return pl.pallas_call(
        functools.partial(_ffn1_kernel, nw=nw),
        out_shape=jax.ShapeDtypeStruct((grp, t, f), bf16 if nw == 1 else f32),
        grid=(grp, t // tm, f // tn),
        in_specs=[pl.BlockSpec((1, tm, d), lambda g_, i, j: (g_, i, 0)),
                  pl.BlockSpec((1, d), lambda g_, i, j: (0, 0)),
                  _mod_specs(t, tm, d, per_row), _mod_specs(t, tm, d, per_row)]
        + [pl.BlockSpec((d, tn), lambda g_, i, j: (0, j))] * (2 * nw),
        out_specs=pl.BlockSpec((1, tm, tn), lambda g_, i, j: (g_, i, j)),
        scratch_shapes=[pltpu.VMEM((tm, d), bf16)] * nw,
        compiler_params=_cp(("arbitrary", "arbitrary", "arbitrary")),
        name="ffn1",
    )(x, g, shift, scale, *wgs, *wus)


def _router_kernel(x_ref, g_ref, sh_ref, sc_ref, whi_ref, wlo_ref, h_ref, r_ref):
    h = _modulated(x_ref, g_ref, sh_ref, sc_ref)
    h_ref[0] = h
    hi, lo = _split2(h)
    logits = _dot(hi, whi_ref[...]) + _dot(lo, whi_ref[...]) + _dot(hi, wlo_ref[...])
    lane = lax.broadcasted_iota(i32, logits.shape, 1)
    lg = jnp.where(lane < N_EXPERTS, logits, -jnp.inf)
    v1 = jnp.max(lg, axis=1, keepdims=True)
    i1 = jnp.min(jnp.where(lg == v1, lane, LANES), axis=1, keepdims=True)
    lg2 = jnp.where(lane == i1, -jnp.inf, lg)
    v2 = jnp.max(lg2, axis=1, keepdims=True)
    i2 = jnp.min(jnp.where(lg2 == v2, lane, LANES), axis=1, keepdims=True)
    e2 = jnp.exp(v2 - v1)
    g1 = 1.0 / (1.0 + e2)
    g2 = e2 / (1.0 + e2)
    r_ref[0] = jnp.where(lane == 0, i1.astype(f32), jnp.where(lane == 1, i2.astype(f32),
                         jnp.where(lane == 2, g1, jnp.where(lane == 3, g2, 0.0))))


def router(x, g, shift, scale, whi, wlo):
    grp, t, d = x.shape
    tm = min(t, 512)
    per_row = shift.shape[1] != 1
    mod = (pl.BlockSpec((1, tm, d), lambda g_, i: (g_, i, 0)) if per_row
           else pl.BlockSpec((1, 1, d), lambda g_, i: (g_, 0, 0)))
    return pl.pallas_call(
        _router_kernel,
        out_shape=(jax.ShapeDtypeStruct((grp, t, d), f32), jax.ShapeDtypeStruct((grp, t, LANES), f32)),
        grid=(grp, t // tm),
        in_specs=[pl.BlockSpec((1, tm, d), lambda g_, i: (g_, i, 0)),
                  pl.BlockSpec((1, d), lambda g_, i: (0, 0)), mod, mod,
                  pl.BlockSpec((d, LANES), lambda g_, i: (0, 0)),
                  pl.BlockSpec((d, LANES), lambda g_, i: (0, 0))],
        out_specs=(pl.BlockSpec((1, tm, d), lambda g_, i: (g_, i, 0)),
                   pl.BlockSpec((1, tm, LANES), lambda g_, i: (g_, i, 0))),
        compiler_params=_cp(("arbitrary", "arbitrary")),
        name="router",
    )(x, g, shift, scale, whi, wlo)


def _gather_rows_kernel(idx_ref, src_ref, dst_ref, sem, *, rows):
    base = pl.program_id(0) * rows

    def row_copy(src_row, dst_row):
        return pltpu.make_async_copy(src_ref.at[pl.ds(src_row, 1)], dst_ref.at[pl.ds(dst_row, 1)], sem)

    def start(r, c):
        row_copy(idx_ref[base + r], r).start()
        return c

    def wait(r, c):
        row_copy(0, r).wait()
        return c

    lax.fori_loop(0, rows, start, 0, unroll=8)
    lax.fori_loop(0, rows, wait, 0, unroll=8)


def gather_rows(idx, src, n_out):
    rows = 512 if n_out % 512 == 0 else 128
    return pl.pallas_call(
        functools.partial(_gather_rows_kernel, rows=rows),
        out_shape=jax.ShapeDtypeStruct((n_out, src.shape[1]), src.dtype),
        grid_spec=pltpu.PrefetchScalarGridSpec(
            num_scalar_prefetch=1, grid=(n_out // rows,),
            in_specs=[pl.BlockSpec(memory_space=pl.ANY)],
            out_specs=pl.BlockSpec((rows, src.shape[1]), lambda i, idx_: (i, 0)),
            scratch_shapes=[pltpu.SemaphoreType.DMA]),
        compiler_params=_cp(("arbitrary",)),
        name="gather_rows",
    )(idx, src)


def _moe1_kernel(te_ref, nt_ref, xs_ref, *refs, nw):
    wg_refs, wu_refs, o_ref = refs[:nw], refs[nw:2 * nw], refs[2 * nw]

    @pl.when(pl.program_id(1) < nt_ref[0])
    def _():
        x = xs_ref[...]
        gate = _mm(x, [w[0] for w in wg_refs])
        o_ref[...] = (_silu(gate) * _mm(x, [w[0] for w in wu_refs])).astype(o_ref.dtype)

    @pl.when(pl.program_id(1) >= nt_ref[0])
    def _():
        o_ref[...] = jnp.zeros(o_ref.shape, o_ref.dtype)


def moe1(tile_expert, n_tiles_used, xs, wgs, wus, tm):
    p, d = xs.shape
    fe = wgs[0].shape[2]
    nw = len(wgs)
    tn = fe // 2 if nw == 1 else 256
    return pl.pallas_call(
        functools.partial(_moe1_kernel, nw=nw),
        out_shape=jax.ShapeDtypeStruct((p, fe), bf16 if nw == 1 else f32),
        grid_spec=pltpu.PrefetchScalarGridSpec(
            num_scalar_prefetch=2, grid=(fe // tn, p // tm),
            in_specs=[pl.BlockSpec((tm, d), lambda j, i, te, nt: (i, 0))]
            + [pl.BlockSpec((1, d, tn), lambda j, i, te, nt: (te[i], 0, j))] * (2 * nw),
            out_specs=pl.BlockSpec((tm, tn), lambda j, i, te, nt: (i, j))),
        compiler_params=_cp(("arbitrary", "arbitrary")),
        name="moe1",
    )(tile_expert, n_tiles_used, xs, *wgs, *wus)


def _moe2_kernel(te_ref, nt_ref, a_ref, *refs, nw):
    wd_refs, sg_ref, o_ref = refs[:nw], refs[nw], refs[nw + 1]

    @pl.when(pl.program_id(1) < nt_ref[0])
    def _():
        o_ref[...] = sg_ref[...] * _mm(a_ref[...], [w[0] for w in wd_refs])

    @pl.when(pl.program_id(1) >= nt_ref[0])
    def _():
        o_ref[...] = jnp.zeros(o_ref.shape, o_ref.dtype)


def moe2(tile_expert, n_tiles_used, act, wds, slot_gate, tm):
    p, fe = act.shape
    d = wds[0].shape[2]
    nw = len(wds)
    tn = d // 2 if nw == 1 else 512
    return pl.pallas_call(
        functools.partial(_moe2_kernel, nw=nw),
        out_shape=jax.ShapeDtypeStruct((p, d), f32),
        grid_spec=pltpu.PrefetchScalarGridSpec(
            num_scalar_prefetch=2, grid=(d // tn, p // tm),
            in_specs=[pl.BlockSpec((tm, fe), lambda j, i, te, nt: (i, 0))]
            + [pl.BlockSpec((1, fe, tn), lambda j, i, te, nt: (te[i], 0, j))] * nw
            + [pl.BlockSpec((tm, 1), lambda j, i, te, nt: (i, 0))],
            out_specs=pl.BlockSpec((tm, tn), lambda j, i, te, nt: (i, j))),
        compiler_params=_cp(("arbitrary", "arbitrary")),
        name="moe2",
    )(tile_expert, n_tiles_used, act, *wds, slot_gate)


def _combine_kernel(x_ref, gt_ref, y1_ref, y2_ref, o_ref):
    o_ref[0] = x_ref[0] + gt_ref[0] * (y1_ref[0] + y2_ref[0])


def combine(x, gate, y1, y2):
    grp, t, d = x.shape
    tm = min(t, 512)
    per_row = gate.shape[1] != 1
    row = pl.BlockSpec((1, tm, d), lambda g_, i: (g_, i, 0))
    return pl.pallas_call(
        _combine_kernel,
        out_shape=jax.ShapeDtypeStruct((grp, t, d), f32),
        grid=(grp, t // tm),
        in_specs=[row, row if per_row else pl.BlockSpec((1, 1, d), lambda g_, i: (g_, 0, 0)), row, row],
        out_specs=row,
        compiler_params=_cp(("arbitrary", "arbitrary")),
        name="combine",
    )(x, gate, y1, y2)


def moe_layer(x, g, shift, scale, gate, whi, wlo, wgs, wus, wds):
    grp, t, d = x.shape
    n = grp * t
    tm = 512 if n >= 4096 else 128
    h, route = router(x, g, shift, scale, whi, wlo)
    route = route.reshape(n, LANES)
    e_idx = route[:, 0:TOP_K].astype(i32)
    gates = route[:, 2:2 + TOP_K]
    e_flat = e_idx.reshape(-1)
    onehot = (e_flat[:, None] == jnp.arange(N_EXPERTS, dtype=i32)[None, :]).astype(i32)
    within = jnp.cumsum(onehot, axis=0) - onehot
    counts = jnp.sum(onehot, axis=0)
    padded = (counts + tm - 1) // tm * tm
    starts = jnp.cumsum(padded) - padded
    slot = jnp.sum(onehot * (starts[None, :] + within), axis=1)
    p = TOP_K * n + N_EXPERTS * tm
    p = -(-p // 256) * 256
    slot_tok = jnp.zeros((p,), i32).at[slot].set(jnp.arange(TOP_K * n, dtype=i32) // TOP_K)
    slot_gate = jnp.zeros((p,), f32).at[slot].set(gates.reshape(-1))
    n_tiles = p // tm
    ends = jnp.cumsum(padded)
    tile_start = jnp.arange(n_tiles, dtype=i32) * tm
    tile_expert = jnp.minimum(jnp.sum((tile_start[:, None] >= ends[None, :]).astype(i32), axis=1), N_EXPERTS - 1).astype(i32)
    n_used = (ends[-1] // tm).astype(i32).reshape(1)
    xs = gather_rows(slot_tok, h.reshape(n, d), p)
    act = moe1(tile_expert, n_used, xs, wgs, wus, tm)
    ys = moe2(tile_expert, n_used, act, wds, slot_gate.reshape(p, 1), tm)
    slots = slot.reshape(n, TOP_K)
    y1 = gather_rows(slots[:, 0], ys, n).reshape(grp, t, d)
    y2 = gather_rows(slots[:, 1], ys, n).reshape(grp, t, d)
    return combine(x, gate, y1, y2)


def _rope_tables(pos):
    half = ROPE_DIMS // 2
    inv = ROPE_THETA ** (-(jnp.arange(half, dtype=f32) * 2.0 / ROPE_DIMS))
    ang = pos.astype(f32)[:, None] * inv[None, :]
    cos, sin = jnp.cos(ang), jnp.sin(ang)
    n = pos.shape[0]
    pad = jnp.zeros((n, NSA_DH - ROPE_DIMS), f32)
    zero = jnp.zeros((n, half), f32)
    c64 = jnp.concatenate([cos, cos, pad + 1.0], axis=1)
    a64 = jnp.concatenate([-sin, zero, pad], axis=1)
    b64 = jnp.concatenate([zero, sin, pad], axis=1)
    return tuple(jnp.tile(x, (1, 2)) for x in (c64, a64, b64))


def _hilo(w):
    hi = lax.reduce_precision(w, exponent_bits=8, mantissa_bits=7)
    return hi.astype(bf16), (w - hi).astype(bf16)


def _pair_diag(w):
    z = jnp.zeros_like(w)
    return jnp.concatenate([jnp.concatenate([w, z], axis=-1), jnp.concatenate([z, w], axis=-1)], axis=-2)


def _mods(mod_l, rows, per_row_t):
    d = mod_l.shape[1] // 6
    m = mod_l.reshape(rows, 2, 3, d)
    out = {}
    for si, sname in enumerate(("mix", "ffn")):
        for ti, tname in enumerate(("shift", "scale", "gate")):
            v = m[:, si, ti]
            if per_row_t:
                v = jnp.repeat(v, per_row_t, axis=0)[None]
            else:
                v = v[:, None, :]
            out[sname + "_" + tname] = v
    return out


def kernel(x_prompt, x_sample, cache_kv, state_kv_win, state_gla, page_table, c_prompt, c_sample, norm_g, w_ada, b_ada, w_in, w_alpha2, b_alpha, gla_norm_g, qk_g, phi_pe, phi_w1, phi_w2, w_branch_a, w_branch_b, w_out, w_ffn_gate, w_ffn_up, w_ffn_down, w_router, w_exp_gate, w_exp_up, w_exp_down):
    bp, tp, d = x_prompt.shape
    bs, t_new, _ = x_sample.shape
    depth = w_in.shape[0]
    n_pool, psz = cache_kv.shape[1], cache_kv.shape[2]
    n_pages = page_table.shape[1]
    past_len = n_pages * psz
    wbuf = state_kv_win.shape[2]
    ts = TS_PAD

    w_in_r = _hilo(jnp.concatenate([
        w_in[:, :, 0:2048], w_in[:, :, 2064:3088], w_in[:, :, 3088:4112], w_in[:, :, 4112:5648],
        w_in[:, :, 5648:5696], w_in[:, :, 2048:2064], jnp.zeros((depth, d, C_MG - C_GA - GLA_RANK), f32),
        w_in[:, :, 5696:9792]], axis=2))
    ba2 = b_alpha.reshape(depth, 1, -1)
    wba, wbb, wo = _hilo(w_branch_a), _hilo(w_branch_b), _hilo(w_out)
    wfg, wfu, wfd = _hilo(w_ffn_gate), _hilo(w_ffn_up), _hilo(w_ffn_down)
    weg, weu, wed = _hilo(w_exp_gate), _hilo(w_exp_up), _hilo(w_exp_down)
    wr_hi, wr_lo = _hilo(jnp.pad(w_router, ((0, 0), (0, 0), (0, LANES - N_EXPERTS))))
    w1c = _hilo(_pair_diag(phi_w1.reshape(depth, 2, CMP_BLOCK, NSA_DH, -1)))
    w2c = _hilo(_pair_diag(phi_w2))
    pe2 = jnp.tile(phi_pe, (1, 1, 1, 2))
    gq = jnp.tile(qk_g[:, 0:1, :], (1, 1, 2))
    gk = jnp.tile(qk_g[:, 1:4, :], (1, 1, 2))
    gkc = jnp.tile(qk_g[:, 4:5, :], (1, 1, 2))
    ggl = gla_norm_g.reshape(depth, 1, -1)
    ng = norm_g.reshape(depth, 2, 1, d)

    c_pad = jnp.zeros((16, d), f32).at[:bp].set(c_prompt).at[bp:bp + bs].set(c_sample)
    mod_all = ada_all(c_pad, w_ada, b_ada)

    cos_p, sa_p, sb_p = _rope_tables(jnp.arange(tp, dtype=i32))
    pos_s = past_len + (jnp.arange(bs * ts, dtype=i32) % ts)
    cos_s, sa_s, sb_s = _rope_tables(pos_s)

    page_flat = page_table.reshape(-1).astype(i32)
    cache4 = cache_kv.reshape(depth, n_pool, psz, 4 * NSA_KV_HEADS * NSA_DH)
    win4 = state_kv_win.reshape(depth, bs, wbuf, 2 * NSA_KV_HEADS * NSA_DH)

    x_p = x_prompt
    x_s = jnp.zeros((bs, ts, d), f32).at[:, :t_new].set(x_sample).reshape(1, bs * ts, d)
    kv_p, win_p, gla_p, kv_s, win_s, gla_s = [], [], [], [], [], []
    hi = lambda w, i: (w[0][i],)
    hl = lambda w, i: (w[0][i], w[1][i])
    flat = lambda a: a.reshape(1, bs * ts, -1)
    for l in range(depth):
        mp = _mods(mod_all[l, :bp], bp, 0)
        ms = _mods(mod_all[l, bp:bp + bs], bs, ts)
        z = proj(x_p, ng[l, 0], mp["mix_shift"], mp["mix_scale"], hi(w_in_r, l))
        qz, kvc, kvw = nsa_prep(z, cos_p, sa_p, sb_p, gq[l], gk[l], bf16)
        o_gla, st = gla(z, w_alpha2[l], ba2[l], None, GLA_CHUNK, GLA_SUB, GLA_CHUNK, False)
        kc, vc = compress_prompt(kvc, pe2[l], w1c[0][l], w2c[0][l], gkc[l])
        o_cmp, sel = cmpsel_prompt(qz, kc, vc)
        o_slc = attn_prompt(qz, kvc, sel, "slc")
        o_win = attn_prompt(qz, kvw, None, "win")
        merged = mix1(o_gla, z, o_cmp, o_slc, o_win, ggl[l], hi(wba, l), hi(wbb, l))
        x_p = mm_res(merged, hi(wo, l), x_p, mp["mix_gate"])
        kv_p.append(kvc.reshape(bp, tp, 4, NSA_KV_HEADS, NSA_DH))
        win_p.append(kvw[:, tp - WINDOW:].reshape(bp, WINDOW, 2, NSA_KV_HEADS, NSA_DH))
        gla_p.append(st)
        z = proj(x_s, ng[l, 0], ms["mix_shift"], ms["mix_scale"], hl(w_in_r, l))
        qz, kvc, kvw = nsa_prep(z, cos_s, sa_s, sb_s, gq[l], gk[l], f32)
        z3 = z.reshape(bs, ts, N_PROJ)
        o_gla, st = gla(z3, w_alpha2[l], ba2[l], state_gla[l], ts, ts, t_new, True)
        qz3 = qz.reshape(bs, ts, -1)
        kvc3 = kvc.reshape(bs, ts, -1)
        kvw3 = kvw.reshape(bs, ts, -1)
        kc, vc = compress_sample(page_flat, cache4, l, bs, n_pages, pe2[l], hl(w1c, l), hl(w2c, l), gkc[l])
        o_cmp, sel_t = cmpsel_sample(qz3, kc, vc, past_len)
        o_slc = slc_sample(page_flat, cache4, l, n_pages, qz3, sel_t, kvc3, t_new)
        o_win, new_win = win_sample(qz3, win4, l, kvw3, t_new)
        merged = mix1(flat(o_gla), z, flat(o_cmp), flat(o_slc), flat(o_win), ggl[l], hl(wba, l), hl(wbb, l))
        x_s = mm_res(merged, hl(wo, l), x_s, ms["mix_gate"])
        kv_s.append(kvc3[:, :t_new].reshape(bs, t_new, 4, NSA_KV_HEADS, NSA_DH))
        win_s.append(new_win.reshape(bs, wbuf, 2, NSA_KV_HEADS, NSA_DH))
        gla_s.append(st)
        e = l // 2
        if l % 2 == 0:
            a = ffn1(x_p, ng[l, 1], mp["ffn_shift"], mp["ffn_scale"], hi(wfg, e), hi(wfu, e))
            x_p = mm_res(a, hi(wfd, e), x_p, mp["ffn_gate"])
            a = ffn1(x_s, ng[l, 1], ms["ffn_shift"], ms["ffn_scale"], hl(wfg, e), hl(wfu, e))
            x_s = mm_res(a, hl(wfd, e), x_s, ms["ffn_gate"])
        else:
            x_p = moe_layer(x_p, ng[l, 1], mp["ffn_shift"], mp["ffn_scale"], mp["ffn_gate"],
                            wr_hi[e], wr_lo[e], hi(weg, e), hi(weu, e), hi(wed, e))
            x_s = moe_layer(x_s, ng[l, 1], ms["ffn_shift"], ms["ffn_scale"], ms["ffn_gate"],
                            wr_hi[e], wr_lo[e], hl(weg, e), hl(weu, e), hl(wed, e))
    y_s = x_s.reshape(bs, ts, d)[:, :t_new]
    return (x_p, y_s, jnp.stack(kv_p), jnp.stack(win_p), jnp.stack(gla_p),
            jnp.stack(kv_s), jnp.stack(win_s), jnp.stack(gla_s))
```

```python
import functools

import jax
import jax.numpy as jnp
from jax import lax
from jax.experimental import pallas as pl
from jax.experimental.pallas import tpu as pltpu

f32 = jnp.float32
bf16 = jnp.bfloat16
i32 = jnp.int32

GLA_HEADS = 4
GLA_DK = 128
GLA_DV = 256
GLA_RANK = 16
GLA_TAU = 16.0
GLA_CHUNK = 64
GLA_SUB = 16
NSA_HEADS = 16
NSA_KV_HEADS = 4
NSA_DH = 64
CMP_BLOCK = 64
TOP_N = 16
WINDOW = 512
Q_BLOCK = 128
ROPE_THETA = 500000.0
ROPE_DIMS = NSA_DH // 4
N_EXPERTS = 8
TOP_K = 2
EPS = 1e-6
NEG_INF = -1e30
FORCED_SCORE = 1e4

LANES = 128
TS_PAD = 16
VMEM_LIMIT = 56 * 1024 * 1024

C_GQ, C_GK, C_GV, C_GR, C_NQ, C_KVC, C_KVW, C_NG, C_GA, C_MG = 0, 512, 1024, 2048, 3072, 4096, 5120, 5632, 5680, 6144
N_PROJ = 10240
PAGES_PER_STEP = 8


def _cp(sem):
    return pltpu.CompilerParams(dimension_semantics=sem, vmem_limit_bytes=VMEM_LIMIT)


def _dot(a, b):
    return jnp.dot(a, b, preferred_element_type=f32)


def _dot_nt(a, b):
    return lax.dot_general(a, b, (((1,), (1,)), ((), ())), preferred_element_type=f32)


def _dot_tn(a, b):
    return lax.dot_general(a, b, (((0,), (0,)), ((), ())), preferred_element_type=f32)


def _split2(x):
    hi = x.astype(bf16)
    lo = (x - hi.astype(f32)).astype(bf16)
    return hi, lo


def _split3(x):
    hi = x.astype(bf16)
    r = x - hi.astype(f32)
    mid = r.astype(bf16)
    lo = (r - mid.astype(f32)).astype(bf16)
    return hi, mid, lo


def _dotp(fn, a, b, hp):
    if not hp:
        return fn(a.astype(bf16), b.astype(bf16))
    ah, al = _split2(a)
    bh, bl = _split2(b)
    return fn(ah, bh) + fn(al, bh) + fn(ah, bl)


def _mm(a, ws):
    if len(ws) == 1:
        return _dot(a.astype(bf16), ws[0])
    ah, al = _split2(a)
    return _dot(ah, ws[0]) + _dot(al, ws[0]) + _dot(ah, ws[1])


def _silu(x):
    return x * jax.nn.sigmoid(x)


def _group_ones(n, group):
    r = lax.broadcasted_iota(i32, (n, n), 0) // group
    c = lax.broadcasted_iota(i32, (n, n), 1) // group
    return jnp.where(r == c, 1.0, 0.0).astype(bf16)


def _rms64(x, gain):
    e = _group_ones(LANES, NSA_DH)
    hi, lo = _split2(x * x)
    ssq = _dot(hi, e) + _dot(lo, e)
    return x * lax.rsqrt(ssq * (1.0 / NSA_DH) + EPS) * gain


def _modulated(x_ref, g_ref, sh_ref, sc_ref):
    x = x_ref[0]
    y = x * lax.rsqrt(jnp.mean(x * x, axis=-1, keepdims=True) + EPS) * g_ref[...]
    return y * (1.0 + sc_ref[0]) + sh_ref[0]


def _mod_specs(t, tm, d, per_row):
    if per_row:
        return pl.BlockSpec((1, tm, d), lambda g, i, j: (g, i, 0))
    return pl.BlockSpec((1, 1, d), lambda g, i, j: (g, 0, 0))


def _ada_kernel(c_ref, w_ref, b_ref, o_ref):
    o_ref[0] = _dotp(_dot, c_ref[...], w_ref[0], True) + b_ref[0]


def ada_all(c_pad, w_ada, b_ada):
    nl, d, n = w_ada.shape
    tn = 1024
    return pl.pallas_call(
        _ada_kernel,
        out_shape=jax.ShapeDtypeStruct((nl, c_pad.shape[0], n), f32),
        grid=(nl, n // tn),
        in_specs=[pl.BlockSpec(c_pad.shape, lambda l, j: (0, 0)),
                  pl.BlockSpec((1, d, tn), lambda l, j: (l, 0, j)),
                  pl.BlockSpec((1, 1, tn), lambda l, j: (l, 0, j))],
        out_specs=pl.BlockSpec((1, c_pad.shape[0], tn), lambda l, j: (l, 0, j)),
        compiler_params=_cp(("arbitrary", "arbitrary")),
        name="ada",
    )(c_pad, w_ada, b_ada.reshape(nl, 1, n))


def _store_split(h, h_refs):
    if len(h_refs) == 1:
        h_refs[0][...] = h.astype(bf16)
    else:
        hi, lo = _split2(h)
        h_refs[0][...] = hi
        h_refs[1][...] = lo


def _mm_split(h_refs, w_refs):
    acc = _dot(h_refs[0][...], w_refs[0][...])
    if len(w_refs) == 2:
        acc = acc + _dot(h_refs[1][...], w_refs[0][...]) + _dot(h_refs[0][...], w_refs[1][...])
    return acc


def _proj_kernel(*refs, sig_from, nw):
    x_ref, g_ref, sh_ref, sc_ref = refs[:4]
    w_refs = refs[4:4 + nw]
    o_ref = refs[4 + nw]
    h_refs = refs[5 + nw:]
    j = pl.program_id(2)

    @pl.when(j == 0)
    def _():
        _store_split(_modulated(x_ref, g_ref, sh_ref, sc_ref), h_refs)

    acc = _mm_split(h_refs, w_refs)

    @pl.when(j < sig_from)
    def _():
        o_ref[0] = acc

    @pl.when(j >= sig_from)
    def _():
        o_ref[0] = jax.nn.sigmoid(acc)


def proj(x, g, shift, scale, ws):
    grp, t, d = x.shape
    n = ws[0].shape[1]
    tm, tn = min(t, 1024), 1024
    per_row = shift.shape[1] != 1
    nw = len(ws)
    return pl.pallas_call(
        functools.partial(_proj_kernel, sig_from=C_MG // tn, nw=nw),
        out_shape=jax.ShapeDtypeStruct((grp, t, n), f32),
        grid=(grp, t // tm, n // tn),
        in_specs=[pl.BlockSpec((1, tm, d), lambda g_, i, j: (g_, i, 0)),
                  pl.BlockSpec((1, d), lambda g_, i, j: (0, 0)),
                  _mod_specs(t, tm, d, per_row), _mod_specs(t, tm, d, per_row)]
        + [pl.BlockSpec((d, tn), lambda g_, i, j: (0, j))] * nw,
        out_specs=pl.BlockSpec((1, tm, tn), lambda g_, i, j: (g_, i, j)),
        scratch_shapes=[pltpu.VMEM((tm, d), bf16)] * nw,
        compiler_params=_cp(("arbitrary", "arbitrary", "arbitrary")),
        name="proj",
    )(x, g, shift, scale, *ws)


def _rope(y, cos, sa, sb):
    return y * cos + pltpu.roll(y, LANES - ROPE_DIMS // 2, 1) * sa + pltpu.roll(y, ROPE_DIMS // 2, 1) * sb


def _nsa_prep_kernel(nq_ref, kvc_ref, kvw_ref, cos_ref, sa_ref, sb_ref, gq_ref, gk_ref, qz_ref, oc_ref, ow_ref):
    cos, sa, sb = cos_ref[...], sa_ref[...], sb_ref[...]
    lane = lax.broadcasted_iota(i32, cos.shape, 1)
    left = lane < NSA_DH
    for m in range(NSA_HEADS // 2):
        y = _rope(_rms64(nq_ref[0, :, m * LANES:(m + 1) * LANES], gq_ref[...]), cos, sa, sb) * (NSA_DH ** -0.5)
        yr = pltpu.roll(y, NSA_DH, 1)
        if (m // 2) % 2 == 0:
            a, b = jnp.where(left, y, 0.0), jnp.where(left, yr, 0.0)
        else:
            a, b = jnp.where(left, 0.0, yr), jnp.where(left, 0.0, y)
        qz_ref[0, :, (2 * m) * LANES:(2 * m + 1) * LANES] = a.astype(qz_ref.dtype)
        qz_ref[0, :, (2 * m + 1) * LANES:(2 * m + 2) * LANES] = b.astype(qz_ref.dtype)
    for c in range(8):
        x = kvc_ref[0, :, c * LANES:(c + 1) * LANES]
        if (c // 2) % 2 == 0:
            x = _rope(_rms64(x, gk_ref[c // 4:c // 4 + 1, :]), cos, sa, sb)
        oc_ref[0, :, c * LANES:(c + 1) * LANES] = x
    for c in range(4):
        x = kvw_ref[0, :, c * LANES:(c + 1) * LANES]
        if c < 2:
            x = _rope(_rms64(x, gk_ref[2:3, :]), cos, sa, sb)
        ow_ref[0, :, c * LANES:(c + 1) * LANES] = x


def nsa_prep(z, cos, sa, sb, gq, gk, q_dtype):
    grp, t, _ = z.shape
    tm = min(t, 256)
    tab = pl.BlockSpec((tm, LANES), lambda g_, i: (i, 0))
    return pl.pallas_call(
        _nsa_prep_kernel,
        out_shape=(jax.ShapeDtypeStruct((grp, t, 2 * NSA_HEADS * NSA_DH), q_dtype),
                   jax.ShapeDtypeStruct((grp, t, 1024), f32),
                   jax.ShapeDtypeStruct((grp, t, 512), f32)),
        grid=(grp, t // tm),
        in_specs=[pl.BlockSpec((1, tm, 1024), lambda g_, i: (g_, i, C_NQ // 1024)),
                  pl.BlockSpec((1, tm, 1024), lambda g_, i: (g_, i, C_KVC // 1024)),
                  pl.BlockSpec((1, tm, 512), lambda g_, i: (g_, i, C_KVW // 512)),
                  tab, tab, tab,
                  pl.BlockSpec((1, LANES), lambda g_, i: (0, 0)),
                  pl.BlockSpec((3, LANES), lambda g_, i: (0, 0))],
        out_specs=(pl.BlockSpec((1, tm, 2 * NSA_HEADS * NSA_DH), lambda g_, i: (g_, i, 0)),
                   pl.BlockSpec((1, tm, 1024), lambda g_, i: (g_, i, 0)),
                   pl.BlockSpec((1, tm, 512), lambda g_, i: (g_, i, 0))),
        compiler_params=_cp(("arbitrary", "arbitrary")),
        name="nsa_prep",
    )(z, z, z, cos, sa, sb, gq, gk)


def _gla_kernel(*refs, chunk, sub, n_valid, has_init, hp):
    if has_init:
        q_ref, k_ref, v_ref, ga_ref, wa_ref, ba_ref, s0_ref, o_ref, so_ref, st_ref = refs
    else:
        q_ref, k_ref, v_ref, ga_ref, wa_ref, ba_ref, o_ref, so_ref, st_ref = refs
    i = pl.program_id(1)
    tb = q_ref.shape[1]
    ns = chunk // sub

    @pl.when(i == 0)
    def _():
        for h in range(GLA_HEADS):
            if has_init:
                st_ref[h] = s0_ref[0, h].T
            else:
                st_ref[h] = jnp.zeros((GLA_DV, GLA_DK), f32)

    ri = lax.broadcasted_iota(i32, (chunk, chunk), 0)
    ci = lax.broadcasted_iota(i32, (chunk, chunk), 1)
    ci_sub = lax.broadcasted_iota(i32, (sub, chunk), 1)
    tri =jnp.where(ri >= ci, 1.0, 0.0).astype(bf16)
    rrow = lax.broadcasted_iota(i32, (chunk, GLA_DK), 0)
    rmod = rrow % sub
    ga_off = C_GA % LANES

    def do_chunk(c, carry):
        r0 = pl.multiple_of(c * chunk, chunk)
        ga = ga_ref[0, pl.ds(r0, chunk), :][:, ga_off:ga_off + GLA_RANK]
        for h in range(GLA_HEADS):
            hs = slice(h * GLA_DK, (h + 1) * GLA_DK)
            q = q_ref[0, pl.ds(r0, chunk), hs] * (GLA_DK ** -0.5)
            k = k_ref[0, pl.ds(r0, chunk), hs]
            v = v_ref[0, pl.ds(r0, chunk), h * GLA_DV:(h + 1) * GLA_DV]
            x = _dotp(_dot, ga, wa_ref[:, hs], hp) + ba_ref[:, hs]
            la = (jnp.minimum(x, 0.0) - jnp.log1p(jnp.exp(-jnp.abs(x)))) * (1.0 / GLA_TAU)
            if n_valid < chunk:
                la = jnp.where(rrow < n_valid, la, 0.0)
                k = jnp.where(rrow < n_valid, k, 0.0)
            l1, l2, l3 = _split3(la)
            cum = _dot(tri, l1) + _dot(tri, l2) + _dot(tri, l3)
            parts = [jnp.zeros((sub, chunk), f32)]
            for sb_i in range(1, ns):
                lo = sb_i * sub
                m_i = cum[lo:lo + 1, :]
                q_i = q[lo:lo + sub] * jnp.exp(cum[lo:lo + sub] - m_i)
                k_i = k * jnp.exp(jnp.minimum(m_i - cum, 0.0))
                parts.append(jnp.where(ci_sub < lo, _dotp(_dot_nt, q_i, k_i, hp), 0.0))
            scores = parts[0] if ns == 1 else jnp.concatenate(parts, axis=0)
            for j in range(min(sub, n_valid)):
                kj = [jnp.broadcast_to(k[s * sub + j:s * sub + j + 1, :], (sub, GLA_DK)) for s in range(ns)]
                cj = [jnp.broadcast_to(cum[s * sub + j:s * sub + j + 1, :], (sub, GLA_DK)) for s in range(ns)]
                kj = kj[0] if ns == 1 else jnp.concatenate(kj, axis=0)
                cj = cj[0] if ns == 1 else jnp.concatenate(cj, axis=0)
                e = jnp.exp(jnp.where(rmod >= j, cum - cj, NEG_INF))
                val = jnp.sum(q * kj * e, axis=1, keepdims=True)
                scores = jnp.where(ci == (ri // sub) * sub + j, val, scores)
            st = st_ref[h]
            o = _dotp(_dot, scores, v, hp) + _dotp(_dot_nt, q * jnp.exp(cum), st, hp)
            o_ref[0, pl.ds(r0, chunk), h * GLA_DV:(h + 1) * GLA_DV] = o
            last = cum[chunk - 1:chunk, :]
            st_ref[h] = st * jnp.exp(last) + _dotp(_dot_tn, v, k * jnp.exp(last - cum), hp)
        return carry

    lax.fori_loop(0, tb // chunk, do_chunk, 0)

    @pl.when(i == pl.num_programs(1) - 1)
    def _():
        for h in range(GLA_HEADS):
            so_ref[0, h] = st_ref[h].T


def gla(z, wa, ba, s0, chunk, sub, n_valid, hp):
    b, t, _ = z.shape
    tb = min(t, 256)
    has_init = s0 is not None
    hk = GLA_HEADS * GLA_DK
    in_specs = [pl.BlockSpec((1, tb, hk), lambda b_, i: (b_, i, C_GQ // hk)),
                pl.BlockSpec((1, tb, hk), lambda b_, i: (b_, i, C_GK // hk)),
                pl.BlockSpec((1, tb, GLA_HEADS * GLA_DV), lambda b_, i: (b_, i, C_GV // (GLA_HEADS * GLA_DV))),
                pl.BlockSpec((1, tb, LANES), lambda b_, i: (b_, i, C_GA // LANES)),
                pl.BlockSpec((GLA_RANK, hk), lambda b_, i: (0, 0)),
                pl.BlockSpec((1, hk), lambda b_, i: (0, 0))]
    args = [z, z, z, z, wa, ba]
    if has_init:
        in_specs.append(pl.BlockSpec((1, GLA_HEADS, GLA_DK, GLA_DV), lambda b_, i: (b_, 0, 0, 0)))
        args.append(s0)
    return pl.pallas_call(
        functools.partial(_gla_kernel, chunk=chunk, sub=sub, n_valid=n_valid, has_init=has_init, hp=hp),
        out_shape=(jax.ShapeDtypeStruct((b, t, GLA_HEADS * GLA_DV), f32),
                   jax.ShapeDtypeStruct((b, GLA_HEADS, GLA_DK, GLA_DV), f32)),
        grid=(b, t // tb),
        in_specs=in_specs,
        out_specs=(pl.BlockSpec((1, tb, GLA_HEADS * GLA_DV), lambda b_, i: (b_, i, 0)),
                   pl.BlockSpec((1, GLA_HEADS, GLA_DK, GLA_DV), lambda b_, i: (b_, 0, 0, 0))),
        scratch_shapes=[pltpu.VMEM((GLA_HEADS, GLA_DV, GLA_DK), f32)],
        compiler_params=_cp(("arbitrary", "arbitrary")),
        name="gla",
    )(*args)


def _compress_core(x_refs, pe_ref, w1_refs, w2_refs, gk_ref, kc_ref, vc_ref, nblk):
    for c in range(4):
        br = c // 2

        def body(s, acc, c=c, br=br):
            rows = x_refs[c][pl.ds(s, nblk, stride=CMP_BLOCK), :] + pe_ref[br, pl.ds(s, 1), :]
            return acc + _mm(rows, [w[br, s] for w in w1_refs])

        acc = lax.fori_loop(0, CMP_BLOCK, body, jnp.zeros((nblk, 2 * LANES), f32), unroll=8)
        y = _mm(_silu(acc), [w[br] for w in w2_refs])
        if br == 0:
            kc_ref[0, :, (c % 2) * LANES:(c % 2 + 1) * LANES] = _rms64(y, gk_ref[...])
        else:
            vc_ref[0, :, (c % 2) * LANES:(c % 2 + 1) * LANES] = y


def _compress_prompt_kernel(x0, x1, x2, x3, pe_ref, w1_ref, w2_ref, gk_ref, kc_ref, vc_ref):
    nblk = x0.shape[1] // CMP_BLOCK
    _compress_core([x0.at[0], x1.at[0], x2.at[0], x3.at[0]], pe_ref, (w1_ref,), (w2_ref,), gk_ref, kc_ref, vc_ref, nblk)


def compress_prompt(kvc, pe2, w1c, w2c, gkc):
    b, t, _ = kvc.shape
    nblk = t // CMP_BLOCK
    cst = lambda shape: pl.BlockSpec(shape, lambda b_: (0,) * len(shape))
    xs = [pl.BlockSpec((1, t, LANES), functools.partial(lambda b_, c: (b_, 0, c), c=c)) for c in range(4)]
    return pl.pallas_call(
        _compress_prompt_kernel,
        out_shape=(jax.ShapeDtypeStruct((b, nblk, 256), f32), jax.ShapeDtypeStruct((b, nblk, 256), f32)),
        grid=(b,),
        in_specs=xs + [cst(pe2.shape), cst(w1c.shape), cst(w2c.shape), cst(gkc.shape)],
        out_specs=(pl.BlockSpec((1, nblk, 256), lambda b_: (b_, 0, 0)), pl.BlockSpec((1, nblk, 256), lambda b_: (b_, 0, 0))),
        compiler_params=_cp(("arbitrary",)),
        name="compress_prompt",
    )(kvc, kvc, kvc, kvc, pe2, w1c, w2c, gkc)


def _compress_sample_kernel(pt_ref, *refs, pages_macro):
    pages = refs[:PAGES_PER_STEP]
    pe_ref, w1h_ref, w1l_ref, w2h_ref, w2l_ref, gk_ref, kc_ref, vc_ref, x0, x1, x2, x3 = refs[PAGES_PER_STEP:]
    st = pl.program_id(2)
    xs = [x0, x1, x2, x3]
    psz = pages[0].shape[2]
    for kk in range(PAGES_PER_STEP):
        r0 = pl.multiple_of((st * PAGES_PER_STEP + kk) * psz, psz)
        for c in range(4):
            xs[c][pl.ds(r0, psz), :] = pages[kk][0, 0, :, c * LANES:(c + 1) * LANES]

    @pl.when(st == pl.num_programs(2) - 1)
    def _():
        _compress_core(xs, pe_ref, (w1h_ref, w1l_ref), (w2h_ref, w2l_ref), gk_ref, kc_ref, vc_ref,
                       pages_macro * psz // CMP_BLOCK)


def compress_sample(page_flat, cache4, layer, nb, n_pages, pe2, w1cs, w2cs, gkc):
    psz = cache4.shape[2]
    pages_macro = min(n_pages, 128)
    n_macro = n_pages // pages_macro
    steps = pages_macro // PAGES_PER_STEP
    nblk_macro = pages_macro * psz // CMP_BLOCK
    nblk = n_pages * psz // CMP_BLOCK

    def page_map(b_, m, s, pt, kk):
        return (layer, pt[b_ * n_pages + m * pages_macro + s * PAGES_PER_STEP + kk], 0, 0)

    cst = lambda shape: pl.BlockSpec(shape, lambda b_, m, s, pt: (0,) * len(shape), pipeline_mode=pl.Buffered(1))
    pspecs = [pl.BlockSpec((1, 1, psz, 512), functools.partial(page_map, kk=kk)) for kk in range(PAGES_PER_STEP)]
    ospec = pl.BlockSpec((1, nblk_macro, 256), lambda b_, m, s, pt: (b_, m, 0))
    return pl.pallas_call(
        functools.partial(_compress_sample_kernel, pages_macro=pages_macro),
        out_shape=(jax.ShapeDtypeStruct((nb, nblk, 256), f32), jax.ShapeDtypeStruct((nb, nblk, 256), f32)),
        grid_spec=pltpu.PrefetchScalarGridSpec(
            num_scalar_prefetch=1, grid=(nb, n_macro, steps),
            in_specs=pspecs + [cst(pe2.shape), cst(w1cs[0].shape), cst(w1cs[1].shape),
                               cst(w2cs[0].shape), cst(w2cs[1].shape), cst(gkc.shape)],
            out_specs=(ospec, ospec),
            scratch_shapes=[pltpu.VMEM((pages_macro * psz, LANES), f32) for _ in range(4)]),
        compiler_params=_cp(("arbitrary", "arbitrary", "arbitrary")),
        name="compress_sample",
    )(page_flat, *([cache4] * PAGES_PER_STEP), pe2, *w1cs, *w2cs, gkc)


def _stack_heads(qz, rows):
    return jnp.concatenate([qz[:, h * LANES:(h + 1) * LANES] for h in range(8)], axis=0)


def _unstack_heads(o, rows):
    lane = lax.broadcasted_iota(i32, (rows, LANES), 1)
    left = lane < NSA_DH
    outs = []
    for p in range(4):
        a = o[(2 * p) * rows:(2 * p + 1) * rows]
        b = o[(2 * p + 1) * rows:(2 * p + 2) * rows]
        if p < 2:
            outs.append(jnp.where(left, a, pltpu.roll(b, NSA_DH, 1)))
        else:
            outs.append(jnp.where(left, pltpu.roll(a, NSA_DH, 1), b))
    return jnp.concatenate(outs, axis=1)


def _rank_loop(sc_ref, nblk):
    shape = sc_ref.shape
    blk = lax.broadcasted_iota(i32, shape, 0)
    sc = sc_ref[...]

    def body(j, rank):
        rj = jnp.broadcast_to(sc_ref[pl.ds(j, 1), :], shape)
        beats = (rj > sc) | ((rj == sc) & (blk > j))
        return rank + jnp.where(beats, 1, 0)

    return lax.fori_loop(0, nblk, body, jnp.zeros(shape, i32), unroll=8)


def _cmpsel_prompt_kernel(qz_ref, kc_ref, vc_ref, oc_ref, sel_ref, sc_ref):
    qb = pl.program_id(1)
    tq = qz_ref.shape[1]
    nf = kc_ref.shape[1]
    tpos = qb * tq + lax.broadcasted_iota(i32, (8 * tq, nf), 0) % tq
    blk_end = (lax.broadcasted_iota(i32, (8 * tq, nf), 1) + 1) * CMP_BLOCK - 1
    vis = blk_end <= tpos
    for kp in range(2):
        q = _stack_heads(qz_ref[0, :, kp * 8 * LANES:(kp + 1) * 8 * LANES], tq)
        kc = kc_ref[0, :, kp * LANES:(kp + 1) * LANES].astype(bf16)
        vc = vc_ref[0, :, kp * LANES:(kp + 1) * LANES].astype(bf16)
        s = jnp.where(vis, _dot_nt(q, kc), NEG_INF)
        e = jnp.where(vis, jnp.exp(s - jnp.max(s, axis=1, keepdims=True)), 0.0)
        p = e / jnp.maximum(jnp.sum(e, axis=1, keepdims=True), 1e-30)
        oc_ref[0, :, kp * 4 * LANES:(kp + 1) * 4 * LANES] = _unstack_heads(_dot(p.astype(bf16), vc), tq)
        imp_a = p[0:tq] + p[tq:2 * tq] + p[2 * tq:3 * tq] + p[3 * tq:4 * tq]
        imp_b = p[4 * tq:5 * tq] + p[5 * tq:6 * tq] + p[6 * tq:7 * tq] + p[7 * tq:8 * tq]
        imp_t = jnp.concatenate([imp_a, imp_b], axis=1).T
        blk = lax.broadcasted_iota(i32, (nf, tq), 0)
        cur = (qb * tq + lax.broadcasted_iota(i32, (nf, tq), 1)) // CMP_BLOCK
        forced = (blk == 0) | (blk == cur) | (blk == cur - 1)
        sels = []
        for half in range(2):
            sc = jnp.where(blk <= cur, jnp.where(forced, FORCED_SCORE, imp_t[half * nf:(half + 1) * nf]), NEG_INF)
            sc_ref[...] = sc
            rank = _rank_loop(sc_ref, nf)
            sels.append(jnp.where((rank < TOP_N) & (sc > 0.5 * NEG_INF), 1.0, 0.0))
        sel_ref[0, :, kp * 2 * nf:(kp + 1) * 2 * nf] = jnp.concatenate(sels, axis=0).T.astype(bf16)


def cmpsel_prompt(qz, kc, vc):
    b, t, _ = qz.shape
    nf = kc.shape[1]
    tq = Q_BLOCK
    return pl.pallas_call(
        _cmpsel_prompt_kernel,
        out_shape=(jax.ShapeDtypeStruct((b, t, NSA_HEADS * NSA_DH), f32),
                   jax.ShapeDtypeStruct((b, t, NSA_KV_HEADS * nf), bf16)),
        grid=(b, t // tq),
        in_specs=[pl.BlockSpec((1, tq, 2 * NSA_HEADS * NSA_DH), lambda b_, i: (b_, i, 0)),
                  pl.BlockSpec((1, nf, 256), lambda b_, i: (b_, 0, 0)),
                  pl.BlockSpec((1, nf, 256), lambda b_, i: (b_, 0, 0))],
        out_specs=(pl.BlockSpec((1, tq, NSA_HEADS * NSA_DH), lambda b_, i: (b_, i, 0)),
                   pl.BlockSpec((1, tq, NSA_KV_HEADS * nf), lambda b_, i: (b_, i, 0))),
        scratch_shapes=[pltpu.VMEM((nf, tq), f32)],
        compiler_params=_cp(("arbitrary", "arbitrary")),
        name="cmpsel_prompt",
    )(qz, kc, vc)


def _attn_prompt_kernel(*refs, mode, kt):
    if mode == "slc":
        qz_ref, k_ref, v_ref, sel_ref, o_ref, m_ref, l_ref, acc_ref = refs
    else:
        qz_ref, k_ref, v_ref, o_ref, m_ref, l_ref, acc_ref = refs
    qb = pl.program_id(2)
    tq = qz_ref.shape[1]
    q = _stack_heads(qz_ref[0], tq)
    tpos = qb * tq + lax.broadcasted_iota(i32, (tq, kt), 0)
    kiota = lax.broadcasted_iota(i32, (tq, kt), 1)
    m_ref[...] = jnp.full(m_ref.shape, NEG_INF, f32)
    l_ref[...] = jnp.zeros(l_ref.shape, f32)
    acc_ref[...] = jnp.zeros(acc_ref.shape, f32)

    def tile(k0):
        k = k_ref[0, pl.ds(k0, kt), :].astype(bf16)
        v = v_ref[0, pl.ds(k0, kt), :].astype(bf16)
        kpos = k0 + kiota
        if mode == "slc":
            nf = sel_ref.shape[2] // 2
            er = lax.broadcasted_iota(i32, (2 * nf, kt), 0)
            ec = (k0 + lax.broadcasted_iota(i32, (2 * nf, kt), 1)) // CMP_BLOCK
            sel = sel_ref[0]
            ma = _dot(sel, jnp.where(er == ec, 1.0, 0.0).astype(bf16))
            mb = _dot(sel, jnp.where(er - nf == ec, 1.0, 0.0).astype(bf16))
            causal = kpos <= tpos
            biases = (jnp.where(causal & (ma > 0.5), 0.0, NEG_INF), jnp.where(causal & (mb > 0.5), 0.0, NEG_INF))
        else:
            ba = jnp.where((kpos <= tpos) & (kpos >= tpos - WINDOW), 0.0, NEG_INF)
            biases = (ba, ba)
        s_all = _dot_nt(q, k)
        heads = [s_all[h * tq:(h + 1) * tq] + biases[h // 4] for h in range(8)]
        m_cur = jnp.concatenate([jnp.broadcast_to(jnp.max(s, axis=1, keepdims=True), (tq, LANES)) for s in heads], axis=0)
        m_old = m_ref[...]
        m_new = jnp.maximum(m_old, m_cur)
        alpha = jnp.exp(m_old - m_new)
        m_ref[...] = m_new
        ps, sums = [], []
        for h in range(8):
            m_h = m_new[h * tq:(h + 1) * tq]
            p = jnp.exp(heads[h] - jnp.concatenate([m_h] * (kt // LANES), axis=1))
            sums.append(jnp.broadcast_to(jnp.sum(p, axis=1, keepdims=True), (tq, LANES)))
            ps.append(p.astype(bf16))
        l_ref[...] = alpha * l_ref[...] + jnp.concatenate(sums, axis=0)
        acc_ref[...] = alpha * acc_ref[...] + _dot(jnp.concatenate(ps, axis=0), v)

    if mode == "slc":
        def body(t_i, c):
            tile(pl.multiple_of(t_i * kt, kt))
            return c
        lax.fori_loop(0, (qb * tq + tq + kt - 1) // kt, body, 0)
    else:
        tile(pl.multiple_of(jnp.maximum(qb * tq - WINDOW, 0), tq))
    o_ref[0] = _unstack_heads(acc_ref[...] / l_ref[...], tq)


def attn_prompt(qz, kv, sel, mode):
    b, t, _ = qz.shape
    tq = Q_BLOCK
    if mode == "slc":
        kt, kcol, vcol = 512, 4, 6
    else:
        kt, kcol, vcol = WINDOW + tq, 0, 2
        assert t >= kt
    in_specs = [pl.BlockSpec((1, tq, 8 * LANES), lambda b_, kp, i: (b_, i, kp)),
                pl.BlockSpec((1, t, LANES), lambda b_, kp, i: (b_, 0, kcol + kp)),
                pl.BlockSpec((1, t, LANES), lambda b_, kp, i: (b_, 0, vcol + kp))]
    args = [qz, kv, kv]
    if mode == "slc":
        nf2 = sel.shape[2] // 2
        in_specs.append(pl.BlockSpec((1, tq, nf2), lambda b_, kp, i: (b_, i, kp)))
        args.append(sel)
    return pl.pallas_call(
        functools.partial(_attn_prompt_kernel, mode=mode, kt=kt),
        out_shape=jax.ShapeDtypeStruct((b, t, NSA_HEADS * NSA_DH), f32),
        grid=(b, 2, t // tq),
        in_specs=in_specs,
        out_specs=pl.BlockSpec((1, tq, 4 * LANES), lambda b_, kp, i: (b_, i, kp)),
        scratch_shapes=[pltpu.VMEM((8 * tq, LANES), f32)] * 3,
        compiler_params=_cp(("arbitrary", "arbitrary", "arbitrary")),
        name="attn_" + mode,
    )(*args)


def _cmpsel_sample_kernel(qz_ref, kc_ref, vc_ref, oc_ref, sel_ref, sc_ref, *, q_pos0):
    ts = qz_ref.shape[1]
    nf = kc_ref.shape[1]
    nfp = sc_ref.shape[0]
    rows = 8 * ts
    tpos = q_pos0 + lax.broadcasted_iota(i32, (rows, nf), 0) % ts
    vis = (lax.broadcasted_iota(i32, (rows, nf), 1) + 1) * CMP_BLOCK - 1 <= tpos
    for kp in range(2):
        q = _stack_heads(qz_ref[0, :, kp * 8 * LANES:(kp + 1) * 8 * LANES], ts)
        kc = kc_ref[0, :, kp * LANES:(kp + 1) * LANES]
        vc = vc_ref[0, :, kp * LANES:(kp + 1) * LANES]
        s = jnp.where(vis, _dotp(_dot_nt, q, kc, True), NEG_INF)
        e = jnp.where(vis, jnp.exp(s - jnp.max(s, axis=1, keepdims=True)), 0.0)
        p = e / jnp.maximum(jnp.sum(e, axis=1, keepdims=True), 1e-30)
        oc_ref[0, :, kp * 4 * LANES:(kp + 1) * 4 * LANES] = _unstack_heads(_dotp(_dot, p, vc, True), ts)
        imp_a = p[0:ts] + p[ts:2 * ts] + p[2 * ts:3 * ts] + p[3 * ts:4 * ts]
        imp_b = p[4 * ts:5 * ts] + p[5 * ts:6 * ts] + p[6 * ts:7 * ts] + p[7 * ts:8 * ts]
        imp = jnp.concatenate([imp_a] * 4 + [imp_b] * 4, axis=0)
        if nfp > nf:
            imp = jnp.concatenate([imp, jnp.zeros((rows, nfp - nf), f32)], axis=1)
        imp_t = jnp.concatenate([imp[:, c * LANES:(c + 1) * LANES].T for c in range(nfp // LANES)], axis=0)
        blk = lax.broadcasted_iota(i32, (nfp, rows), 0)
        cur = (q_pos0 + lax.broadcasted_iota(i32, (nfp, rows), 1) % ts) // CMP_BLOCK
        forced = (blk == 0) | (blk == cur) | (blk == cur - 1)
        sc = jnp.where(blk < nf, jnp.where(forced, FORCED_SCORE, imp_t), NEG_INF)
        sc_ref[...] = sc
        rank = _rank_loop(sc_ref, nf)
        sel_ref[0, kp] = jnp.where((rank < TOP_N - 1) & (sc > 0.5 * NEG_INF), 1.0, 0.0)


def cmpsel_sample(qz3, kc, vc, q_pos0):
    nb, ts, _ = qz3.shape
    nf = kc.shape[1]
    nfp = -(-nf // LANES) * LANES
    return pl.pallas_call(
        functools.partial(_cmpsel_sample_kernel, q_pos0=q_pos0),
        out_shape=(jax.ShapeDtypeStruct((nb, ts, NSA_HEADS * NSA_DH), f32),
                   jax.ShapeDtypeStruct((nb, 2, nfp, 8 * ts), f32)),
        grid=(nb,),
        in_specs=[pl.BlockSpec((1, ts, 2 * NSA_HEADS * NSA_DH), lambda b_: (b_, 0, 0)),
                  pl.BlockSpec((1, nf, 256), lambda b_: (b_, 0, 0)),
                  pl.BlockSpec((1, nf, 256), lambda b_: (b_, 0, 0))],
        out_specs=(pl.BlockSpec((1, ts, NSA_HEADS * NSA_DH), lambda b_: (b_, 0, 0)),
                   pl.BlockSpec((1, 2, nfp, 8 * ts), lambda b_: (b_, 0, 0, 0))),
        scratch_shapes=[pltpu.VMEM((nfp, 8 * ts), f32)],
        compiler_params=_cp(("arbitrary",)),
        name="cmpsel_sample",
    )(qz3, kc, vc)


def _slc_sample_kernel(pt_ref, *refs, n_new):
    npg = PAGES_PER_STEP
    kpages, vpages = refs[:npg], refs[npg:2 * npg]
    qz_ref, sel_ref, kn_ref, vn_ref, o_ref, m_ref, l_ref, acc_ref = refs[2 * npg:]
    st = pl.program_id(1)
    ts = qz_ref.shape[1]
    cols = 8 * ts

    @pl.when(st == 0)
    def _():
        m_ref[...] = jnp.full(m_ref.shape, NEG_INF, f32)
        l_ref[...] = jnp.zeros(l_ref.shape, f32)
        acc_ref[...] = jnp.zeros(acc_ref.shape, f32)

    qs = []
    for kp in range(2):
        qh, ql = _split2(_stack_heads(qz_ref[0, :, kp * 8 * LANES:(kp + 1) * 8 * LANES], ts))
        qs.append((qh, jnp.concatenate([qh, ql], axis=0)))

    def update(kp, k, v, bias):
        qh, q2 = qs[kp]
        kh, kl = _split2(k)
        s2 = _dot_nt(kh, q2)
        s = s2[:, :cols] + s2[:, cols:] + _dot_nt(kl, qh) + bias
        m_old = m_ref[kp]
        m_new = jnp.maximum(m_old, jnp.max(s, axis=0, keepdims=True))
        p = jnp.exp(s - m_new)
        alpha = jnp.exp(m_old - m_new)
        l_ref[kp] = alpha * l_ref[kp] + jnp.sum(p, axis=0, keepdims=True)
        vh, vl = _split2(v)
        ph, pl_ = _split2(p)
        pv2 = _dot_tn(vh, jnp.concatenate([ph, pl_], axis=1))
        acc_ref[kp] = alpha * acc_ref[kp] + pv2[:, :cols] + pv2[:, cols:] + _dot_tn(vl, ph)
        m_ref[kp] = m_new

    k = jnp.concatenate([r[0, 0] for r in kpages], axis=0)
    v = jnp.concatenate([r[0, 0] for r in vpages], axis=0)
    psz = kpages[0].shape[2]
    bps = npg * psz // CMP_BLOCK
    er = lax.broadcasted_iota(i32, (bps * CMP_BLOCK, bps), 0) // CMP_BLOCK
    ec = lax.broadcasted_iota(i32, (bps * CMP_BLOCK, bps), 1)
    expand = jnp.where(er == ec, 1.0, 0.0).astype(bf16)
    for kp in range(2):
        msk = _dot(expand, sel_ref[0, kp].astype(bf16))
        update(kp, k[:, kp * LANES:(kp + 1) * LANES], v[:, kp * LANES:(kp + 1) * LANES],
               jnp.where(msk > 0.5, 0.0, NEG_INF))

    @pl.when(st == pl.num_programs(1) - 1)
    def _():
        j = lax.broadcasted_iota(i32, (ts, cols), 0)
        tok = lax.broadcasted_iota(i32, (ts, cols), 1) % ts
        bias_n = jnp.where((j <= tok) & (j < n_new), 0.0, NEG_INF)
        for kp in range(2):
            update(kp, kn_ref[0, :, kp * LANES:(kp + 1) * LANES], vn_ref[0, :, kp * LANES:(kp + 1) * LANES], bias_n)
            o = (acc_ref[kp] / l_ref[kp]).T
            o_ref[0, :, kp * 4 * LANES:(kp + 1) * 4 * LANES] = _unstack_heads(o, ts)


def slc_sample(page_flat, cache4, layer, n_pages, qz3, sel_t, kvc3, n_new):
    nb, ts, _ = qz3.shape
    psz = cache4.shape[2]
    steps = n_pages // PAGES_PER_STEP
    bps = PAGES_PER_STEP * psz // CMP_BLOCK
    cols = 8 * ts

    def page_map(b_, s, pt, kk, col):
        return (layer, pt[b_ * n_pages + s * PAGES_PER_STEP + kk], 0, col)

    kspecs = [pl.BlockSpec((1, 1, psz, 2 * LANES), functools.partial(page_map, kk=kk, col=2)) for kk in range(PAGES_PER_STEP)]
    vspecs = [pl.BlockSpec((1, 1, psz, 2 * LANES), functools.partial(page_map, kk=kk, col=3)) for kk in range(PAGES_PER_STEP)]
    return pl.pallas_call(
        functools.partial(_slc_sample_kernel, n_new=n_new),
        out_shape=jax.ShapeDtypeStruct((nb, ts, NSA_HEADS * NSA_DH), f32),
        grid_spec=pltpu.PrefetchScalarGridSpec(
            num_scalar_prefetch=1, grid=(nb, steps),
            in_specs=kspecs + vspecs + [
                pl.BlockSpec((1, ts, 16 * LANES), lambda b_, s, pt: (b_, 0, 0)),
                pl.BlockSpec((1, 2, bps, cols), lambda b_, s, pt: (b_, 0, s, 0)),
                pl.BlockSpec((1, ts, 2 * LANES), lambda b_, s, pt: (b_, 0, 2)),
                pl.BlockSpec((1, ts, 2 * LANES), lambda b_, s, pt: (b_, 0, 3))],
            out_specs=pl.BlockSpec((1, ts, 8 * LANES), lambda b_, s, pt: (b_, 0, 0)),
            scratch_shapes=[pltpu.VMEM((2, 1, cols), f32), pltpu.VMEM((2, 1, cols), f32),
                            pltpu.VMEM((2, LANES, cols), f32)]),
        compiler_params=_cp(("arbitrary", "arbitrary")),
        name="slc_sample",
    )(page_flat, *([cache4] * (2 * PAGES_PER_STEP)), qz3, sel_t, kvc3, kvc3)


def _win_sample_kernel(qz_ref, old_ref, new_ref, o_ref, nw_ref, *, n_new):
    ts = qz_ref.shape[1]
    wb = old_ref.shape[2]
    rows = 8 * ts
    old = old_ref[0, 0]
    new = new_ref[0]
    tok_o = lax.broadcasted_iota(i32, (rows, wb), 0) % ts
    j_o = lax.broadcasted_iota(i32, (rows, wb), 1)
    bias_o = jnp.where((j_o >= tok_o + (wb - WINDOW)), 0.0, NEG_INF)
    tok_n = lax.broadcasted_iota(i32, (rows, ts), 0) % ts
    j_n = lax.broadcasted_iota(i32, (rows, ts), 1)
    bias_n = jnp.where((j_n <= tok_n) & (j_n < n_new), 0.0, NEG_INF)
    for kp in range(2):
        q = _stack_heads(qz_ref[0, :, kp * 8 * LANES:(kp + 1) * 8 * LANES], ts)
        ko = old[:, kp * LANES:(kp + 1) * LANES]
        vo = old[:, 256 + kp * LANES:256 + (kp + 1) * LANES]
        kn = new[:, kp * LANES:(kp + 1) * LANES]
        vn = new[:, 256 + kp * LANES:256 + (kp + 1) * LANES]
        so = _dotp(_dot_nt, q, ko, True) + bias_o
        sn = _dotp(_dot_nt, q, kn, True) + bias_n
        m = jnp.maximum(jnp.max(so, axis=1, keepdims=True), jnp.max(sn, axis=1, keepdims=True))
        po, pn = jnp.exp(so - m), jnp.exp(sn - m)
        den = jnp.sum(po, axis=1, keepdims=True) + jnp.sum(pn, axis=1, keepdims=True)
        o = (_dotp(_dot, po, vo, True) + _dotp(_dot, pn, vn, True)) / den
        o_ref[0, :, kp * 4 * LANES:(kp + 1) * 4 * LANES] = _unstack_heads(o, ts)
    both = jnp.concatenate([old, new], axis=0)
    nw_ref[0, 0] = pltpu.roll(both, wb + ts - n_new, 0)[0:wb]


def win_sample(qz3, state_win4, layer, kvw3, n_new):
    nb, ts, _ = qz3.shape
    wb = state_win4.shape[2]
    return pl.pallas_call(
        functools.partial(_win_sample_kernel, n_new=n_new),
        out_shape=(jax.ShapeDtypeStruct((nb, ts, NSA_HEADS * NSA_DH), f32),
                   jax.ShapeDtypeStruct((1, nb, wb, 512), f32)),
        grid=(nb,),
        in_specs=[pl.BlockSpec((1, ts, 2 * NSA_HEADS * NSA_DH), lambda b_: (b_, 0, 0)),
                  pl.BlockSpec((1, 1, wb, 512), lambda b_: (layer, b_, 0, 0)),
                  pl.BlockSpec((1, ts, 512), lambda b_: (b_, 0, 0))],
        out_specs=(pl.BlockSpec((1, ts, NSA_HEADS * NSA_DH), lambda b_: (b_, 0, 0)),
                   pl.BlockSpec((1, 1, wb, 512), lambda b_: (0, b_, 0, 0))),
        compiler_params=_cp(("arbitrary",)),
        name="win_sample",
    )(qz3, state_win4, kvw3)


def _mix1_kernel(*refs, nw):
    og_ref, r_ref, ng_ref, mg0_ref, mg1_ref, oc_ref, os_ref, ow_ref, gg_ref = refs[:9]
    wa_refs, wb_refs = refs[9:9 + nw], refs[9 + nw:9 + 2 * nw]
    o_ref = refs[9 + 2 * nw]
    a_refs, b_refs = refs[10 + 2 * nw:10 + 3 * nw], refs[10 + 3 * nw:]
    j = pl.program_id(2)

    @pl.when(j == 0)
    def _():
        ys = []
        for h in range(GLA_HEADS):
            hs = slice(h * GLA_DV, (h + 1) * GLA_DV)
            x = og_ref[0, :, hs]
            y = x * lax.rsqrt(jnp.mean(x * x, axis=-1, keepdims=True) + EPS) * gg_ref[:, hs]
            ys.append(y * _silu(r_ref[0, :, hs]))
        _store_split(jnp.concatenate(ys, axis=1), a_refs)
        ghi, glo = _split2(jax.nn.sigmoid(ng_ref[0]))
        nh = NSA_HEADS * NSA_DH
        er = lax.broadcasted_iota(i32, (LANES, nh), 0)
        ec = lax.broadcasted_iota(i32, (LANES, nh), 1) // NSA_DH
        acc = None
        for br, ref in enumerate((oc_ref, os_ref, ow_ref)):
            ex = jnp.where(er == br * NSA_HEADS + ec, 1.0, 0.0).astype(bf16)
            term = (_dot(ghi, ex) + _dot(glo, ex)) * ref[0]
            acc = term if acc is None else acc + term
        _store_split(acc, b_refs)

    o_ref[0] = (mg0_ref[0] * _mm_split(a_refs, wa_refs) + mg1_ref[0] * _mm_split(b_refs, wb_refs)).astype(o_ref.dtype)


def mix1(o_gla, z, o_cmp, o_slc, o_win, gg, was, wbs):
    grp, t, n1 = o_gla.shape
    d = was[0].shape[1]
    nw = len(was)
    tm, tn = min(t, 512), 1024
    row = lambda w, c: pl.BlockSpec((1, tm, w), functools.partial(lambda g_, i, j, c: (g_, i, c), c=c))
    return pl.pallas_call(
        functools.partial(_mix1_kernel, nw=nw),
        out_shape=jax.ShapeDtypeStruct((grp, t, d), bf16 if nw == 1 else f32),
        grid=(grp, t // tm, d // tn),
        in_specs=[row(n1, 0), row(1024, C_GR // 1024), row(LANES, C_NG // LANES),
                  pl.BlockSpec((1, tm, tn), lambda g_, i, j: (g_, i, C_MG // tn + j)),
                  pl.BlockSpec((1, tm, tn), lambda g_, i, j: (g_, i, (C_MG + d) // tn + j)),
                  row(n1, 0), row(n1, 0), row(n1, 0),
                  pl.BlockSpec((1, n1), lambda g_, i, j: (0, 0))]
        + [pl.BlockSpec((n1, tn), lambda g_, i, j: (0, j))] * (2 * nw),
        out_specs=pl.BlockSpec((1, tm, tn), lambda g_, i, j: (g_, i, j)),
        scratch_shapes=[pltpu.VMEM((tm, n1), bf16)] * (2 * nw),
        compiler_params=_cp(("arbitrary", "arbitrary", "arbitrary")),
        name="mix1",
    )(o_gla, z, z, z, z, o_cmp, o_slc, o_win, gg, *was, *wbs)


def _mm_res_kernel(*refs, nw):
    a_ref = refs[0]
    w_refs = refs[1:1 + nw]
    x_ref, gt_ref, o_ref = refs[1 + nw:]
    o_ref[0] = x_ref[0] + gt_ref[0] * _mm(a_ref[0], [w[...] for w in w_refs])


def mm_res(a, ws, x, gate):
    grp, t, k = a.shape
    d = ws[0].shape[1]
    nw = len(ws)
    tm, tn = min(t, 1024), 512
    per_row = gate.shape[1] != 1
    return pl.pallas_call(
        functools.partial(_mm_res_kernel, nw=nw),
        out_shape=jax.ShapeDtypeStruct((grp, t, d), f32),
        grid=(grp, t // tm, d // tn),
        in_specs=[pl.BlockSpec((1, tm, k), lambda g_, i, j: (g_, i, 0))]
        + [pl.BlockSpec((k, tn), lambda g_, i, j: (0, j))] * nw
        + [pl.BlockSpec((1, tm, tn), lambda g_, i, j: (g_, i, j)),
           pl.BlockSpec((1, tm, tn), lambda g_, i, j: (g_, i, j)) if per_row
           else pl.BlockSpec((1, 1, tn), lambda g_, i, j: (g_, 0, j))],
        out_specs=pl.BlockSpec((1, tm, tn), lambda g_, i, j: (g_, i, j)),
        compiler_params=_cp(("arbitrary", "arbitrary", "arbitrary")),
        name="mm_res",
    )(a, *ws, x, gate)


def _ffn1_kernel(*refs, nw):
    x_ref, g_ref, sh_ref, sc_ref = refs[:4]
    wg_refs, wu_refs = refs[4:4 + nw], refs[4 + nw:4 + 2 * nw]
    o_ref = refs[4 + 2 * nw]
    h_refs = refs[5 + 2 * nw:]

    @pl.when(pl.program_id(2) == 0)
    def _():
        _store_split(_modulated(x_ref, g_ref, sh_ref, sc_ref), h_refs)

    o_ref[0] = (_silu(_mm_split(h_refs, wg_refs)) * _mm_split(h_refs, wu_refs)).astype(o_ref.dtype)


def ffn1(x, g, shift, scale, wgs, wus):
    grp, t, d = x.shape
    f = wgs[0].shape[1]
    nw = len(wgs)
    tm, tn = min(t, 1024), 512
    per_row = shift.shape[1] != 1
    return pl.pallas_call(
        functools.partial(_ffn1_kernel, nw=nw),
        out_shape=jax.ShapeDtypeStruct((grp, t, f), bf16 if nw == 1 else f32),
        grid=(grp, t // tm, f // tn),
        in_specs=[pl.BlockSpec((1, tm, d), lambda g_, i, j: (g_, i, 0)),
                  pl.BlockSpec((1, d), lambda g_, i, j: (0, 0)),
                  _mod_specs(t, tm, d, per_row), _mod_specs(t, tm, d, per_row)]
        + [pl.BlockSpec((d, tn), lambda g_, i, j: (0, j))] * (2 * nw),
        out_specs=pl.BlockSpec((1, tm, tn), lambda g_, i, j: (g_, i, j)),
        scratch_shapes=[pltpu.VMEM((tm, d), bf16)] * nw,
        compiler_params=_cp(("arbitrary", "arbitrary", "arbitrary")),
        name="ffn1",
    )(x, g, shift, scale, *wgs, *wus)


def _router_kernel(x_ref, g_ref, sh_ref, sc_ref, whi_ref, wlo_ref, h_ref, r_ref):
    h = _modulated(x_ref, g_ref, sh_ref, sc_ref)
    h_ref[0] = h
    hi, lo = _split2(h)
    logits = _dot(hi, whi_ref[...]) + _dot(lo, whi_ref[...]) + _dot(hi, wlo_ref[...])
    lane = lax.broadcasted_iota(i32, logits.shape, 1)
    lg = jnp.where(lane < N_EXPERTS, logits, -jnp.inf)
    v1 = jnp.max(lg, axis=1, keepdims=True)
    i1 = jnp.min(jnp.where(lg == v1, lane, LANES), axis=1, keepdims=True)
    lg2 = jnp.where(lane == i1, -jnp.inf, lg)
    v2 = jnp.max(lg2, axis=1, keepdims=True)
    i2 = jnp.min(jnp.where(lg2 == v2, lane, LANES), axis=1, keepdims=True)
    e2 = jnp.exp(v2 - v1)
    g1 = 1.0 / (1.0 + e2)
    g2 = e2 / (1.0 + e2)
    r_ref[0] = jnp.where(lane == 0, i1.astype(f32), jnp.where(lane == 1, i2.astype(f32),
                         jnp.where(lane == 2, g1, jnp.where(lane == 3, g2, 0.0))))


def router(x, g, shift, scale, whi, wlo):
    grp, t, d = x.shape
    tm = min(t, 512)
    per_row = shift.shape[1] != 1
    mod = (pl.BlockSpec((1, tm, d), lambda g_, i: (g_, i, 0)) if per_row
           else pl.BlockSpec((1, 1, d), lambda g_, i: (g_, 0, 0)))
    return pl.pallas_call(
        _router_kernel,
        out_shape=(jax.ShapeDtypeStruct((grp, t, d), f32), jax.ShapeDtypeStruct((grp, t, LANES), f32)),
        grid=(grp, t // tm),
        in_specs=[pl.BlockSpec((1, tm, d), lambda g_, i: (g_, i, 0)),
                  pl.BlockSpec((1, d), lambda g_, i: (0, 0)), mod, mod,
                  pl.BlockSpec((d, LANES), lambda g_, i: (0, 0)),
                  pl.BlockSpec((d, LANES), lambda g_, i: (0, 0))],
        out_specs=(pl.BlockSpec((1, tm, d), lambda g_, i: (g_, i, 0)),
                   pl.BlockSpec((1, tm, LANES), lambda g_, i: (g_, i, 0))),
        compiler_params=_cp(("arbitrary", "arbitrary")),
        name="router",
    )(x, g, shift, scale, whi, wlo)


def _gather_rows_kernel(idx_ref, src_ref, dst_ref, sem, *, rows):
    base = pl.program_id(0) * rows

    def row_copy(src_row, dst_row):
        return pltpu.make_async_copy(src_ref.at[pl.ds(src_row, 1)], dst_ref.at[pl.ds(dst_row, 1)], sem)

    def start(r, c):
        row_copy(idx_ref[base + r], r).start()
        return c

    def wait(r, c):
        row_copy(0, r).wait()
        return c

    lax.fori_loop(0, rows, start, 0, unroll=8)
    lax.fori_loop(0, rows, wait, 0, unroll=8)


def gather_rows(idx, src, n_out):
    rows = 512 if n_out % 512 == 0 else 128
    return pl.pallas_call(
        functools.partial(_gather_rows_kernel, rows=rows),
        out_shape=jax.ShapeDtypeStruct((n_out, src.shape[1]), src.dtype),
        grid_spec=pltpu.PrefetchScalarGridSpec(
            num_scalar_prefetch=1, grid=(n_out // rows,),
            in_specs=[pl.BlockSpec(memory_space=pl.ANY)],
            out_specs=pl.BlockSpec((rows, src.shape[1]), lambda i, idx_: (i, 0)),
            scratch_shapes=[pltpu.SemaphoreType.DMA]),
        compiler_params=_cp(("arbitrary",)),
        name="gather_rows",
    )(idx, src)


def _moe1_kernel(te_ref, nt_ref, xs_ref, *refs, nw):
    wg_refs, wu_refs, o_ref = refs[:nw], refs[nw:2 * nw], refs[2 * nw]

    @pl.when(pl.program_id(1) < nt_ref[0])
    def _():
        x = xs_ref[...]
        gate = _mm(x, [w[0] for w in wg_refs])
        o_ref[...] = (_silu(gate) * _mm(x, [w[0] for w in wu_refs])).astype(o_ref.dtype)

    @pl.when(pl.program_id(1) >= nt_ref[0])
    def _():
        o_ref[...] = jnp.zeros(o_ref.shape, o_ref.dtype)


def moe1(tile_expert, n_tiles_used, xs, wgs, wus, tm):
    p, d = xs.shape
    fe = wgs[0].shape[2]
    nw = len(wgs)
    tn = fe // 2 if nw == 1 else 256
    return pl.pallas_call(
        functools.partial(_moe1_kernel, nw=nw),
        out_shape=jax.ShapeDtypeStruct((p, fe), bf16 if nw == 1 else f32),
        grid_spec=pltpu.PrefetchScalarGridSpec(
            num_scalar_prefetch=2, grid=(fe // tn, p // tm),
            in_specs=[pl.BlockSpec((tm, d), lambda j, i, te, nt: (i, 0))]
            + [pl.BlockSpec((1, d, tn), lambda j, i, te, nt: (te[i], 0, j))] * (2 * nw),
            out_specs=pl.BlockSpec((tm, tn), lambda j, i, te, nt: (i, j))),
        compiler_params=_cp(("arbitrary", "arbitrary")),
        name="moe1",
    )(tile_expert, n_tiles_used, xs, *wgs, *wus)


def _moe2_kernel(te_ref, nt_ref, a_ref, *refs, nw):
    wd_refs, sg_ref, o_ref = refs[:nw], refs[nw], refs[nw + 1]

    @pl.when(pl.program_id(1) < nt_ref[0])
    def _():
        o_ref[...] = sg_ref[...] * _mm(a_ref[...], [w[0] for w in wd_refs])

    @pl.when(pl.program_id(1) >= nt_ref[0])
    def _():
        o_ref[...] = jnp.zeros(o_ref.shape, o_ref.dtype)


def moe2(tile_expert, n_tiles_used, act, wds, slot_gate, tm):
    p, fe = act.shape
    d = wds[0].shape[2]
    nw = len(wds)
    tn = d // 2 if nw == 1 else 512
    return pl.pallas_call(
        functools.partial(_moe2_kernel, nw=nw),
        out_shape=jax.ShapeDtypeStruct((p, d), f32),
        grid_spec=pltpu.PrefetchScalarGridSpec(
            num_scalar_prefetch=2, grid=(d // tn, p // tm),
            in_specs=[pl.BlockSpec((tm, fe), lambda j, i, te, nt: (i, 0))]
            + [pl.BlockSpec((1, fe, tn), lambda j, i, te, nt: (te[i], 0, j))] * nw
            + [pl.BlockSpec((tm, 1), lambda j, i, te, nt: (i, 0))],
            out_specs=pl.BlockSpec((tm, tn), lambda j, i, te, nt: (i, j))),
        compiler_params=_cp(("arbitrary", "arbitrary")),
        name="moe2",
    )(tile_expert, n_tiles_used, act, *wds, slot_gate)


def _combine_kernel(x_ref, gt_ref, y1_ref, y2_ref, o_ref):
    o_ref[0] = x_ref[0] + gt_ref[0] * (y1_ref[0] + y2_ref[0])


def combine(x, gate, y1, y2):
    grp, t, d = x.shape
    tm = min(t, 512)
    per_row = gate.shape[1] != 1
    row = pl.BlockSpec((1, tm, d), lambda g_, i: (g_, i, 0))
    return pl.pallas_call(
        _combine_kernel,
        out_shape=jax.ShapeDtypeStruct((grp, t, d), f32),
        grid=(grp, t // tm),
        in_specs=[row, row if per_row else pl.BlockSpec((1, 1, d), lambda g_, i: (g_, 0, 0)), row, row],
        out_specs=row,
        compiler_params=_cp(("arbitrary", "arbitrary")),
        name="combine",
    )(x, gate, y1, y2)


def moe_layer(x, g, shift, scale, gate, whi, wlo, wgs, wus, wds):
    grp, t, d = x.shape
    n = grp * t
    tm = 512 if n >= 4096 else 128
    h, route = router(x, g, shift, scale, whi, wlo)
    route = route.reshape(n, LANES)
    e_idx = route[:, 0:TOP_K].astype(i32)
    gates = route[:, 2:2 + TOP_K]
    e_flat = e_idx.reshape(-1)
    onehot = (e_flat[:, None] == jnp.arange(N_EXPERTS, dtype=i32)[None, :]).astype(i32)
    within = jnp.cumsum(onehot, axis=0) - onehot
    counts = jnp.sum(onehot, axis=0)
    padded = (counts + tm - 1) // tm * tm
    starts = jnp.cumsum(padded) - padded
    slot = jnp.sum(onehot * (starts[None, :] + within), axis=1)
    p = TOP_K * n + N_EXPERTS * tm
    p = -(-p // 256) * 256
    slot_tok = jnp.zeros((p,), i32).at[slot].set(jnp.arange(TOP_K * n, dtype=i32) // TOP_K)
    slot_gate = jnp.zeros((p,), f32).at[slot].set(gates.reshape(-1))
    n_tiles = p // tm
    ends = jnp.cumsum(padded)
    tile_start = jnp.arange(n_tiles, dtype=i32) * tm
    tile_expert = jnp.minimum(jnp.sum((tile_start[:, None] >= ends[None, :]).astype(i32), axis=1), N_EXPERTS - 1).astype(i32)
    n_used = (ends[-1] // tm).astype(i32).reshape(1)
    xs = gather_rows(slot_tok, h.reshape(n, d), p)
    act = moe1(tile_expert, n_used, xs, wgs, wus, tm)
    ys = moe2(tile_expert, n_used, act, wds, slot_gate.reshape(p, 1), tm)
    slots = slot.reshape(n, TOP_K)
    y1 = gather_rows(slots[:, 0], ys, n).reshape(grp, t, d)
    y2 = gather_rows(slots[:, 1], ys, n).reshape(grp, t, d)
    return combine(x, gate, y1, y2)


def _rope_tables(pos):
    half = ROPE_DIMS // 2
    inv = ROPE_THETA ** (-(jnp.arange(half, dtype=f32) * 2.0 / ROPE_DIMS))
    ang = pos.astype(f32)[:, None] * inv[None, :]
    cos, sin = jnp.cos(ang), jnp.sin(ang)
    n = pos.shape[0]
    pad = jnp.zeros((n, NSA_DH - ROPE_DIMS), f32)
    zero = jnp.zeros((n, half), f32)
    c64 = jnp.concatenate([cos, cos, pad + 1.0], axis=1)
    a64 = jnp.concatenate([-sin, zero, pad], axis=1)
    b64 = jnp.concatenate([zero, sin, pad], axis=1)
    return tuple(jnp.tile(x, (1, 2)) for x in (c64, a64, b64))


def _hilo(w):
    hi = lax.reduce_precision(w, exponent_bits=8, mantissa_bits=7)
    return hi.astype(bf16), (w - hi).astype(bf16)


def _pair_diag(w):
    z = jnp.zeros_like(w)
    return jnp.concatenate([jnp.concatenate([w, z], axis=-1), jnp.concatenate([z, w], axis=-1)], axis=-2)


def _mods(mod_l, rows, per_row_t):
    d = mod_l.shape[1] // 6
    m = mod_l.reshape(rows, 2, 3, d)
    out = {}
    for si, sname in enumerate(("mix", "ffn")):
        for ti, tname in enumerate(("shift", "scale", "gate")):
            v = m[:, si, ti]
            if per_row_t:
                v = jnp.repeat(v, per_row_t, axis=0)[None]
            else:
                v = v[:, None, :]
            out[sname + "_" + tname] = v
    return out


def kernel(x_prompt, x_sample, cache_kv, state_kv_win, state_gla, page_table, c_prompt, c_sample, norm_g, w_ada, b_ada, w_in, w_alpha2, b_alpha, gla_norm_g, qk_g, phi_pe, phi_w1, phi_w2, w_branch_a, w_branch_b, w_out, w_ffn_gate, w_ffn_up, w_ffn_down, w_router, w_exp_gate, w_exp_up, w_exp_down):
    bp, tp, d = x_prompt.shape
    bs, t_new, _ = x_sample.shape
    depth = w_in.shape[0]
    n_pool, psz = cache_kv.shape[1], cache_kv.shape[2]
    n_pages = page_table.shape[1]
    past_len = n_pages * psz
    wbuf = state_kv_win.shape[2]
    ts = TS_PAD

    w_in_r = _hilo(jnp.concatenate([
        w_in[:, :, 0:2048], w_in[:, :, 2064:3088], w_in[:, :, 3088:4112], w_in[:, :, 4112:5648],
        w_in[:, :, 5648:5696], w_in[:, :, 2048:2064], jnp.zeros((depth, d, C_MG - C_GA - GLA_RANK), f32),
        w_in[:, :, 5696:9792]], axis=2))
    ba2 = b_alpha.reshape(depth, 1, -1)
    wba, wbb, wo = _hilo(w_branch_a), _hilo(w_branch_b), _hilo(w_out)
    wfg, wfu, wfd = _hilo(w_ffn_gate), _hilo(w_ffn_up), _hilo(w_ffn_down)
    weg, weu, wed = _hilo(w_exp_gate), _hilo(w_exp_up), _hilo(w_exp_down)
    wr_hi, wr_lo = _hilo(jnp.pad(w_router, ((0, 0), (0, 0), (0, LANES - N_EXPERTS))))
    w1c = _hilo(_pair_diag(phi_w1.reshape(depth, 2, CMP_BLOCK, NSA_DH, -1)))
    w2c = _hilo(_pair_diag(phi_w2))
    pe2 = jnp.tile(phi_pe, (1, 1, 1, 2))
    gq = jnp.tile(qk_g[:, 0:1, :], (1, 1, 2))
    gk = jnp.tile(qk_g[:, 1:4, :], (1, 1, 2))
    gkc = jnp.tile(qk_g[:, 4:5, :], (1, 1, 2))
    ggl = gla_norm_g.reshape(depth, 1, -1)
    ng = norm_g.reshape(depth, 2, 1, d)

    c_pad = jnp.zeros((16, d), f32).at[:bp].set(c_prompt).at[bp:bp + bs].set(c_sample)
    mod_all = ada_all(c_pad, w_ada, b_ada)

    cos_p, sa_p, sb_p = _rope_tables(jnp.arange(tp, dtype=i32))
    pos_s = past_len + (jnp.arange(bs * ts, dtype=i32) % ts)
    cos_s, sa_s, sb_s = _rope_tables(pos_s)

    page_flat = page_table.reshape(-1).astype(i32)
    cache4 = cache_kv.reshape(depth, n_pool, psz, 4 * NSA_KV_HEADS * NSA_DH)
    win4 = state_kv_win.reshape(depth, bs, wbuf, 2 * NSA_KV_HEADS * NSA_DH)

    x_p = x_prompt
    x_s = jnp.zeros((bs, ts, d), f32).at[:, :t_new].set(x_sample).reshape(1, bs * ts, d)
    kv_p, win_p, gla_p, kv_s, win_s, gla_s = [], [], [], [], [], []
    hi = lambda w, i: (w[0][i],)
    hl = lambda w, i: (w[0][i], w[1][i])
    flat = lambda a: a.reshape(1, bs * ts, -1)
    for l in range(depth):
        mp = _mods(mod_all[l, :bp], bp, 0)
        ms = _mods(mod_all[l, bp:bp + bs], bs, ts)
        z = proj(x_p, ng[l, 0], mp["mix_shift"], mp["mix_scale"], hi(w_in_r, l))
        qz, kvc, kvw = nsa_prep(z, cos_p, sa_p, sb_p, gq[l], gk[l], bf16)
        o_gla, st = gla(z, w_alpha2[l], ba2[l], None, GLA_CHUNK, GLA_SUB, GLA_CHUNK, False)
        kc, vc = compress_prompt(kvc, pe2[l], w1c[0][l], w2c[0][l], gkc[l])
        o_cmp, sel = cmpsel_prompt(qz, kc, vc)
        o_slc = attn_prompt(qz, kvc, sel, "slc")
        o_win = attn_prompt(qz, kvw, None, "win")
        merged = mix1(o_gla, z, o_cmp, o_slc, o_win, ggl[l], hi(wba, l), hi(wbb, l))
        x_p = mm_res(merged, hi(wo, l), x_p, mp["mix_gate"])
        kv_p.append(kvc.reshape(bp, tp, 4, NSA_KV_HEADS, NSA_DH))
        win_p.append(kvw[:, tp - WINDOW:].reshape(bp, WINDOW, 2, NSA_KV_HEADS, NSA_DH))
        gla_p.append(st)
        z = proj(x_s, ng[l, 0], ms["mix_shift"], ms["mix_scale"], hl(w_in_r, l))
        qz, kvc, kvw = nsa_prep(z, cos_s, sa_s, sb_s, gq[l], gk[l], f32)
        z3 = z.reshape(bs, ts, N_PROJ)
        o_gla, st = gla(z3, w_alpha2[l], ba2[l], state_gla[l], ts, ts, t_new, True)
        qz3 = qz.reshape(bs, ts, -1)
        kvc3 = kvc.reshape(bs, ts, -1)
        kvw3 = kvw.reshape(bs, ts, -1)
        kc, vc = compress_sample(page_flat, cache4, l, bs, n_pages, pe2[l], hl(w1c, l), hl(w2c, l), gkc[l])
        o_cmp, sel_t = cmpsel_sample(qz3, kc, vc, past_len)
        o_slc = slc_sample(page_flat, cache4, l, n_pages, qz3, sel_t, kvc3, t_new)
        o_win, new_win = win_sample(qz3, win4, l, kvw3, t_new)
        merged = mix1(flat(o_gla), z, flat(o_cmp), flat(o_slc), flat(o_win), ggl[l], hl(wba, l), hl(wbb, l))
        x_s = mm_res(merged, hl(wo, l), x_s, ms["mix_gate"])
        kv_s.append(kvc3[:, :t_new].reshape(bs, t_new, 4, NSA_KV_HEADS, NSA_DH))
        win_s.append(new_win.reshape(bs, wbuf, 2, NSA_KV_HEADS, NSA_DH))
        gla_s.append(st)
        e = l // 2
        if l % 2 == 0:
            a = ffn1(x_p, ng[l, 1], mp["ffn_shift"], mp["ffn_scale"], hi(wfg, e), hi(wfu, e))
            x_p = mm_res(a, hi(wfd, e), x_p, mp["ffn_gate"])
            a = ffn1(x_s, ng[l, 1], ms["ffn_shift"], ms["ffn_scale"], hl(wfg, e), hl(wfu, e))
            x_s = mm_res(a, hl(wfd, e), x_s, ms["ffn_gate"])
        else:
            x_p = moe_layer(x_p, ng[l, 1], mp["ffn_shift"], mp["ffn_scale"], mp["ffn_gate"],
                            wr_hi[e], wr_lo[e], hi(weg, e), hi(weu, e), hi(wed, e))
            x_s = moe_layer(x_s, ng[l, 1], ms["ffn_shift"], ms["ffn_scale"], ms["ffn_gate"],
                            wr_hi[e], wr_lo[e], hl(weg, e), hl(weu, e), hl(wed, e))
    y_s = x_s.reshape(bs, ts, d)[:, :t_new]
    return (x_p, y_s, jnp.stack(kv_p), jnp.stack(win_p), jnp.stack(gla_p),
            jnp.stack(kv_s), jnp.stack(win_s), jnp.stack(gla_s))
```
